```python
import math
import jax, jax.numpy as jnp
from jax import lax
import numpy as np

D_MODEL = 2048
BATCH = 4
SEQ = 4096
DEPTH = 2

D_MIX = D_MODEL
A_HEADS = 8
A_QK_DIM = 64
A_V_DIM = 2 * A_QK_DIM
A_WIDTH = A_HEADS * A_V_DIM
B_GROUPS = 4
B_CHUNK = 128
B_WIDTH = D_MIX // 4
B_GROUP_DIM = B_WIDTH // B_GROUPS
C_GROUPS = 4
C_WIDTH = D_MIX - A_WIDTH - B_WIDTH
CONV_K = 3

ROPE_THETA = 10000.0
Q_BLOCK = 128
NORM_EPS = 1e-5
LN_EPS = 1e-5

PROJ_SIZES = (A_HEADS * 2 * A_QK_DIM, A_HEADS * 2 * A_QK_DIM, A_WIDTH, A_WIDTH,
              B_WIDTH, B_WIDTH, B_WIDTH,
              C_WIDTH, C_WIDTH, C_WIDTH, C_WIDTH)
PROJ_COLS = sum(PROJ_SIZES)

kernel_name = "hybrid_diffattn_sgu_shortconv"


def _split_points():
    pts, acc = [], 0
    for s in PROJ_SIZES[:-1]:
        acc += s
        pts.append(acc)
    return pts


def rms_norm(x, w):
    xf = x.astype(jnp.float32)
    y = xf * lax.rsqrt(jnp.mean(xf * xf, axis=-1, keepdims=True) + NORM_EPS)
    return (y * w.astype(jnp.float32)).astype(x.dtype)


def layer_norm(x, g, b):
    xf = x.astype(jnp.float32)
    mu = jnp.mean(xf, axis=-1, keepdims=True)
    xc = xf - mu
    var = jnp.mean(xc * xc, axis=-1, keepdims=True)
    y = xc * lax.rsqrt(var + LN_EPS) * g.astype(jnp.float32) + b.astype(jnp.float32)
    return y.astype(x.dtype)


def rotary_tables(seq, dim, dtype):
    pos = jnp.arange(seq, dtype=jnp.float32)
    inv_freq = ROPE_THETA ** (-jnp.arange(0, dim, 2, dtype=jnp.float32) / dim)
    ang = pos[:, None] * inv_freq[None, :]
    return jnp.cos(ang).astype(dtype), jnp.sin(ang).astype(dtype)


def apply_rotary(x, cos, sin):
    x1, x2 = jnp.split(x, 2, axis=-1)
    c = cos[None, :, None, None, :]
    s = sin[None, :, None, None, :]
    return jnp.concatenate([x1 * c - x2 * s, x2 * c + x1 * s], axis=-1)


def diff_attention(q, k, v, lam, cos, sin):
    b, s, h, _, d = q.shape
    q = apply_rotary(q, cos, sin)
    k = apply_rotary(k, cos, sin)
    scale = d ** -0.5
    nb = s // Q_BLOCK
    q_blocks = q.reshape(b, nb, Q_BLOCK, h, 2, d).swapaxes(0, 1)
    k32 = k.astype(jnp.float32)
    key_pos = jnp.arange(s)
    neg = jnp.finfo(jnp.float32).min

    def one_block(args):
        q_blk, start = args
        sc = jnp.einsum('bqhcd,bkhcd->bhcqk', q_blk.astype(jnp.float32), k32) * scale
        q_pos = start + jnp.arange(Q_BLOCK)
        causal = key_pos[None, :] <= q_pos[:, None]
        sc = jnp.where(causal, sc, neg)
        p = jax.nn.softmax(sc, axis=-1)
        attn = p[:, :, 0] - lam * p[:, :, 1]
        return jnp.einsum('bhqk,bkhe->bqhe', attn.astype(v.dtype), v)

    out = lax.map(one_block, (q_blocks, jnp.arange(nb) * Q_BLOCK))
    return out.swapaxes(0, 1).reshape(b, s, h, v.shape[-1])


def spatial_gating(u, v, ln_g, ln_b, w_s, b_s):
    b, s, _ = v.shape
    v = layer_norm(v, ln_g, ln_b)
    nc = s // B_CHUNK
    vc = v.reshape(b, nc, B_CHUNK, B_GROUPS, B_GROUP_DIM)
    w = jnp.tril(w_s)
    mixed = jnp.einsum('gts,bcsgd->bctgd', w, vc) + b_s.T[None, None, :, :, None]
    return u * mixed.reshape(b, s, B_WIDTH)


def short_conv(xc, bgate, cgate, conv_w):
    z = cgate * xc
    s = z.shape[1]
    zp = jnp.pad(z, ((0, 0), (CONV_K - 1, 0), (0, 0)))
    y = conv_w[0] * zp[:, 0:s]
    for i in range(1, CONV_K):
        y = y + conv_w[i] * zp[:, i:i + s]
    return bgate * y


def hybrid_layer(x, layer_idx, norm_w, w_in, lam_q1, lam_k1, lam_q2, lam_k2, subln_w,
                 sgu_ln_g, sgu_ln_b, w_s, b_s, conv_w, w_out, cos, sin):
    b, s, _ = x.shape
    h = rms_norm(x, norm_w)
    proj = h @ w_in
    (q, k, v, gate_a, u, v_s, gate_b, xc, bgate, cgate, gate_c) = jnp.split(
        proj, _split_points(), axis=-1)

    lam_init = 0.8 - 0.6 * math.exp(-0.3 * layer_idx)
    lam = (jnp.exp(jnp.sum(lam_q1.astype(jnp.float32) * lam_k1.astype(jnp.float32)))
           - jnp.exp(jnp.sum(lam_q2.astype(jnp.float32) * lam_k2.astype(jnp.float32)))
           + lam_init)
    q = q.reshape(b, s, A_HEADS, 2, A_QK_DIM)
    k = k.reshape(b, s, A_HEADS, 2, A_QK_DIM)
    v = v.reshape(b, s, A_HEADS, A_V_DIM)
    ya = diff_attention(q, k, v, lam, cos, sin)
    ya = rms_norm(ya, subln_w) * (1.0 - lam_init)
    ya = ya.reshape(b, s, A_WIDTH) * jax.nn.silu(gate_a)

    yb = spatial_gating(u, v_s, sgu_ln_g, sgu_ln_b, w_s, b_s) * jax.nn.silu(gate_b)

    yc = short_conv(xc, bgate, cgate, conv_w) * jax.nn.silu(gate_c)

    y = jnp.concatenate([ya, yb, yc], axis=-1) @ w_out
    return x + y


def setup_inputs(seed: int = 0) -> dict:
    key = jax.random.key(seed)
    ks = jax.random.split(key, 16)
    f32 = jnp.float32
    nrm = lambda k, shape: jax.random.normal(k, shape, dtype=f32)
    return {
        "x": nrm(ks[0], (BATCH, SEQ, D_MODEL)),
        "norm_w": 1.0 + 0.02 * nrm(ks[1], (DEPTH, D_MODEL)),
        "w_in": nrm(ks[2], (DEPTH, D_MODEL, PROJ_COLS)) * D_MODEL ** -0.5,
        "lam_q1": 0.1 * nrm(ks[3], (DEPTH, A_QK_DIM)),
        "lam_k1": 0.1 * nrm(ks[4], (DEPTH, A_QK_DIM)),
        "lam_q2": 0.1 * nrm(ks[5], (DEPTH, A_QK_DIM)),
        "lam_k2": 0.1 * nrm(ks[6], (DEPTH, A_QK_DIM)),
        "subln_w": 1.0 + 0.02 * nrm(ks[7], (DEPTH, A_V_DIM)),
        "sgu_ln_g": 1.0 + 0.02 * nrm(ks[8], (DEPTH, B_WIDTH)),
        "sgu_ln_b": 0.02 * nrm(ks[9], (DEPTH, B_WIDTH)),
        "w_s": nrm(ks[10], (DEPTH, B_GROUPS, B_CHUNK, B_CHUNK)) * B_CHUNK ** -0.5,
        "b_s": 1.0 + 0.1 * nrm(ks[11], (DEPTH, B_GROUPS, B_CHUNK)),
        "conv_w": nrm(ks[12], (DEPTH, CONV_K, C_WIDTH)) * CONV_K ** -0.5,
        "w_out": nrm(ks[13], (DEPTH, D_MIX, D_MODEL)) * D_MIX ** -0.5,
        "final_norm_w": 1.0 + 0.02 * nrm(ks[14], (D_MODEL,)),
    }


def reference(x, norm_w, w_in, lam_q1, lam_k1, lam_q2, lam_k2, subln_w, sgu_ln_g, sgu_ln_b,
              w_s, b_s, conv_w, w_out, final_norm_w):
    cos, sin = rotary_tables(x.shape[1], A_QK_DIM, x.dtype)
    for l in range(DEPTH):
        x = hybrid_layer(x, l, norm_w[l], w_in[l], lam_q1[l], lam_k1[l], lam_q2[l], lam_k2[l],
                         subln_w[l], sgu_ln_g[l], sgu_ln_b[l], w_s[l], b_s[l], conv_w[l],
                         w_out[l], cos, sin)
    return rms_norm(x, final_norm_w)
```

```python
import functools
import math

import jax
import jax.numpy as jnp
from jax import lax
from jax.experimental import pallas as pl
from jax.experimental.pallas import tpu as pltpu

F32 = jnp.float32
BF16 = jnp.bfloat16

D_MODEL = 2048
A_HEADS = 8
QK_DIM = 64
HEAD_W = 2 * QK_DIM
A_WIDTH = A_HEADS * HEAD_W
B_GROUPS = 4
B_CHUNK = 128
B_WIDTH = 512
C_WIDTH = 512
CONV_K = 3
PROJ_COLS = 4 * A_WIDTH + 3 * B_WIDTH + 4 * C_WIDTH
ROPE_THETA = 10000.0
NORM_EPS = 1e-5
LN_EPS = 1e-5
LOG2E = 1.4426950408889634

LANES = 128
VMEM_LIMIT = 56 * 1024 * 1024

PROJ_TM = 1024
PROJ_TN = 512
NORM_ROWS = 128
ATT_T = 512
MIX_TM = 512
HALO = 8

NEG_BIG = -1e30


def _silu(g):
    return g * jax.nn.sigmoid(g)


def _proj_kernel(x_ref, nw_ref, w_ref, tab_ref, o_ref, h_scr, *, n_qk_tiles):
    j = pl.program_id(1)

    @pl.when(j == 0)
    def _():
        nw = nw_ref[...]

        def norm_rows(r, carry):
            rows = pl.ds(pl.multiple_of(r * NORM_ROWS, NORM_ROWS), NORM_ROWS)
            x = x_ref[rows, :]
            ms = jnp.mean(x * x, axis=-1, keepdims=True)
            h_scr[rows, :] = (x * lax.rsqrt(ms + NORM_EPS) * nw).astype(BF16)
            return carry

        lax.fori_loop(0, PROJ_TM // NORM_ROWS, norm_rows, 0)

    acc = jnp.dot(h_scr[...], w_ref[...], preferred_element_type=F32)

    @pl.when(j < n_qk_tiles)
    def _():
        cos = tab_ref[0, 0]
        sin = tab_ref[0, 1]
        for c in range(PROJ_TN // LANES):
            cols = slice(c * LANES, (c + 1) * LANES)
            a = acc[:, cols]
            o_ref[:, cols] = (a * cos + pltpu.roll(a, LANES // 2, 1) * sin).astype(BF16)

    @pl.when(j >= n_qk_tiles)
    def _():
        o_ref[...] = acc.astype(BF16)


def _project(x2, norm_w, w_bf16, tables, seq):
    n_rows = x2.shape[0]
    n_qk_tiles = (2 * A_WIDTH) // PROJ_TN
    pos_blocks = seq // PROJ_TM

    def tab_map(i, j):
        return (jnp.minimum(j // (n_qk_tiles // 2), 1), 0, i % pos_blocks, 0)

    return pl.pallas_call(
        functools.partial(_proj_kernel, n_qk_tiles=n_qk_tiles),
        out_shape=jax.ShapeDtypeStruct((n_rows, PROJ_COLS), BF16),
        grid=(n_rows // PROJ_TM, PROJ_COLS // PROJ_TN),
        in_specs=[
            pl.BlockSpec((PROJ_TM, D_MODEL), lambda i, j: (i, 0)),
            pl.BlockSpec((1, D_MODEL), lambda i, j: (0, 0)),
            pl.BlockSpec((D_MODEL, PROJ_TN), lambda i, j: (0, j)),
            pl.BlockSpec((1, 2, PROJ_TM, LANES), tab_map),
        ],
        out_specs=pl.BlockSpec((PROJ_TM, PROJ_TN), lambda i, j: (i, j)),
        scratch_shapes=[pltpu.VMEM((PROJ_TM, D_MODEL), BF16)],
        compiler_params=pltpu.CompilerParams(
            dimension_semantics=("arbitrary", "arbitrary"), vmem_limit_bytes=VMEM_LIMIT),
        name="proj",
    )(x2, norm_w, w_bf16, tables)


def _attn_kernel(lamv_ref, sw_ref, q_ref, k_ref, v_ref, g_ref, o_ref, vt_scr, acc1_scr, acc2_scr,
                 *, lam_init, seq):
    t = ATT_T
    qi = pl.program_id(2)

    @pl.when(qi == 0)
    def _():
        for c in range(seq // t):
            vt_scr[c] = v_ref[0, c * t:(c + 1) * t, :].astype(F32).T.astype(BF16)

    lane = lax.broadcasted_iota(jnp.int32, (t, HEAD_W), 1)
    is_c1 = (lane % (2 * 32)) < 32
    qf = q_ref[0].astype(F32)
    q1 = jnp.where(is_c1, qf, 0.0).astype(BF16)
    q2 = jnp.where(is_c1, 0.0, qf).astype(BF16)

    acc1_scr[...] = jnp.zeros_like(acc1_scr)
    acc2_scr[...] = jnp.zeros_like(acc2_scr)

    def tile(kt, carry, masked):
        kb = k_ref[0, pl.ds(pl.multiple_of(kt * t, t), t), :]
        vtb = vt_scr[kt]
        new = []
        for (m_old, l_old), qm, acc_ref in ((carry[0:2], q1, acc1_scr), (carry[2:4], q2, acc2_scr)):
            s = lax.dot_general(kb, qm, (((1,), (1,)), ((), ())), preferred_element_type=F32)
            if masked:
                row = lax.broadcasted_iota(jnp.int32, (t, t), 0)
                col = lax.broadcasted_iota(jnp.int32, (t, t), 1)
                s = jnp.where(row <= col, s, NEG_BIG)
            m_new = jnp.maximum(m_old, jnp.max(s, axis=0, keepdims=True))
            alpha = jnp.exp2(m_old - m_new)
            p = jnp.exp2(s - m_new)
            l_new = alpha * l_old + jnp.sum(p, axis=0, keepdims=True)
            acc_ref[...] = alpha * acc_ref[...] + jnp.dot(vtb, p.astype(BF16), preferred_element_type=F32)
            new += [m_new, l_new]
        return tuple(new)

    init = (jnp.full((1, t), NEG_BIG, F32), jnp.zeros((1, t), F32)) * 2
    carry = lax.fori_loop(0, qi, functools.partial(tile, masked=False), init)
    _, l1, _, l2 = tile(qi, carry, masked=True)

    lamv = lamv_ref[...]
    lam = (jnp.exp(jnp.sum(lamv[0:1] * lamv[1:2], axis=-1, keepdims=True))
           - jnp.exp(jnp.sum(lamv[2:3] * lamv[3:4], axis=-1, keepdims=True)) + lam_init)
    d = acc1_scr[...] * (1.0 / l1) - lam * (acc2_scr[...] * (1.0 / l2))
    ms = jnp.mean(d * d, axis=0, keepdims=True)
    y = (d * lax.rsqrt(ms + NORM_EPS)).T
    g = g_ref[0].astype(F32)
    o_ref[0] = (y * (sw_ref[...] * (1.0 - lam_init)) * _silu(g)).astype(BF16)


def _attention(proj3, lamv, subln_w, lam_init):
    b, seq, _ = proj3.shape
    t = ATT_T
    blk = lambda off, full: pl.BlockSpec(
        (1, seq if full else t, HEAD_W),
        (lambda bi, h, qi: (bi, 0, off + h)) if full else (lambda bi, h, qi: (bi, qi, off + h)))
    return pl.pallas_call(
        functools.partial(_attn_kernel, lam_init=lam_init, seq=seq),
        out_shape=jax.ShapeDtypeStruct((b, seq, A_WIDTH), BF16),
        grid=(b, A_HEADS, seq // t),
        in_specs=[
            pl.BlockSpec((4, QK_DIM), lambda bi, h, qi: (0, 0)),
            pl.BlockSpec((1, HEAD_W), lambda bi, h, qi: (0, 0)),
            blk(0, False),
            blk(A_HEADS, True),
            blk(2 * A_HEADS, True),
            blk(3 * A_HEADS, False),
        ],
        out_specs=pl.BlockSpec((1, t, HEAD_W), lambda bi, h, qi: (bi, qi, h)),
        scratch_shapes=[
            pltpu.VMEM((seq // t, HEAD_W, t), BF16),
            pltpu.VMEM((HEAD_W, t), F32),
            pltpu.VMEM((HEAD_W, t), F32),
        ],
        compiler_params=pltpu.CompilerParams(
            dimension_semantics=("arbitrary", "arbitrary", "arbitrary"), vmem_limit_bytes=VMEM_LIMIT),
        name="diff_attn",
    )(lamv, subln_w, proj3, proj3, proj3, proj3)


def _mix_kernel(ya_ref, u_ref, vs_ref, gb_ref, xc_ref, bg_ref, cg_ref, gc_ref, xcp_ref, cgp_ref,
                lng_ref, lnb_ref, ws_ref, bs_ref, cw_ref, wout_ref, x_ref, fnw_ref, o_ref,
                cat_scr, z_scr, *, is_last, tiles_per_seq):
    tm = MIX_TM
    i = pl.program_id(0)

    v = vs_ref[...].astype(F32)
    mu = jnp.mean(v, axis=-1, keepdims=True)
    vc = v - mu
    var = jnp.mean(vc * vc, axis=-1, keepdims=True)
    vn = (vc * lax.rsqrt(var + LN_EPS) * lng_ref[...] + lnb_ref[...]).astype(BF16)
    row = lax.broadcasted_iota(jnp.int32, (B_CHUNK, B_CHUNK), 0)
    col = lax.broadcasted_iota(jnp.int32, (B_CHUNK, B_CHUNK), 1)
    for g in range(B_GROUPS):
        cols = slice(g * LANES, (g + 1) * LANES)
        w = jnp.where(row >= col, ws_ref[g], 0.0).astype(BF16)
        bias = bs_ref[g]
        for c in range(tm // B_CHUNK):
            rows = slice(c * B_CHUNK, (c + 1) * B_CHUNK)
            mixed = jnp.dot(w, vn[rows, cols], preferred_element_type=F32) + bias
            yb = u_ref[rows, cols].astype(F32) * mixed * _silu(gb_ref[rows, cols].astype(F32))
            cat_scr[rows, cols] = yb.astype(BF16)

    seq_start = (i % tiles_per_seq) == 0
    zp = cgp_ref[...].astype(F32) * xcp_ref[...].astype(F32)
    z_scr[0:HALO, :] = jnp.where(seq_start, 0.0, zp)
    z = cg_ref[...].astype(F32) * xc_ref[...].astype(F32)
    z_scr[HALO:HALO + tm, :] = z
    cw = cw_ref[...]
    conv = cw[2:3] * z
    for tap in range(CONV_K - 1):
        shift = CONV_K - 1 - tap
        conv = conv + cw[tap:tap + 1] * z_scr[HALO - shift:HALO - shift + tm, :]
    yc = bg_ref[...].astype(F32) * conv * _silu(gc_ref[...].astype(F32))
    cat_scr[:, B_WIDTH:B_WIDTH + C_WIDTH] = yc.astype(BF16)

    y = jnp.dot(ya_ref[...], wout_ref[0:A_WIDTH, :], preferred_element_type=F32)
    y = y + jnp.dot(cat_scr[...], wout_ref[A_WIDTH:, :], preferred_element_type=F32)
    out = x_ref[...] + y
    if is_last:
        ms = jnp.mean(out * out, axis=-1, keepdims=True)
        out = out * lax.rsqrt(ms + NORM_EPS) * fnw_ref[...]
    o_ref[...] = out


def _mix_out(ya2, proj2, x2, ln_g, ln_b, w_s, bs_b, conv_w, wout_bf16, fnw, seq, is_last):
    n_rows = x2.shape[0]
    tm = MIX_TM
    base = (4 * A_WIDTH) // B_WIDTH
    pblk = lambda k: pl.BlockSpec((tm, B_WIDTH), lambda i: (i, base + k))
    halo = lambda k: pl.BlockSpec((HALO, B_WIDTH), lambda i: (jnp.maximum(i * (tm // HALO) - 1, 0), base + k))
    const = lambda shape: pl.BlockSpec(shape, lambda i: (0,) * len(shape))
    return pl.pallas_call(
        functools.partial(_mix_kernel, is_last=is_last, tiles_per_seq=seq // tm),
        out_shape=jax.ShapeDtypeStruct((n_rows, D_MODEL), F32),
        grid=(n_rows // tm,),
        in_specs=[
            pl.BlockSpec((tm, A_WIDTH), lambda i: (i, 0)),
            pblk(0), pblk(1), pblk(2),
            pblk(3), pblk(4), pblk(5), pblk(6),
            halo(3), halo(5),
            const((1, B_WIDTH)), const((1, B_WIDTH)),
            const((B_GROUPS, B_CHUNK, B_CHUNK)),
            const((B_GROUPS, B_CHUNK, LANES)),
            const((CONV_K, C_WIDTH)),
            const((D_MODEL, D_MODEL)),
            pl.BlockSpec((tm, D_MODEL), lambda i: (i, 0)),
            const((1, D_MODEL)),
        ],
        out_specs=pl.BlockSpec((tm, D_MODEL), lambda i: (i, 0)),
        scratch_shapes=[
            pltpu.VMEM((tm, B_WIDTH + C_WIDTH), BF16),
            pltpu.VMEM((HALO + tm, C_WIDTH), F32),
        ],
        compiler_params=pltpu.CompilerParams(
            dimension_semantics=("arbitrary",), vmem_limit_bytes=VMEM_LIMIT),
        name="mix_out",
    )(ya2, proj2, proj2, proj2, proj2, proj2, proj2, proj2, proj2, proj2,
      ln_g, ln_b, w_s, bs_b, conv_w, wout_bf16, x2, fnw)


def _permute_head_cols(w):
    d = w.shape[0]
    return w.reshape(d, A_HEADS, 2, 2, QK_DIM // 2).transpose(0, 1, 3, 2, 4).reshape(d, A_WIDTH)


def _rotary_tables(seq):
    pos = jnp.arange(seq, dtype=F32)
    inv_freq = ROPE_THETA ** (-jnp.arange(0, QK_DIM, 2, dtype=F32) / QK_DIM)
    ang = pos[:, None] * inv_freq[None, :]
    cos = jnp.tile(jnp.cos(ang), (1, LANES // (QK_DIM // 2)))
    sin = jnp.tile(jnp.sin(ang), (1, LANES // (QK_DIM // 2)))
    sign = jnp.where(jnp.arange(LANES) < LANES // 2, -1.0, 1.0).astype(F32)
    k_tab = jnp.stack([cos, sin * sign])
    q_tab = k_tab * (QK_DIM ** -0.5 * LOG2E)
    return jnp.stack([q_tab, k_tab])


def kernel(x, norm_w, w_in, lam_q1, lam_k1, lam_q2, lam_k2, subln_w, sgu_ln_g, sgu_ln_b, w_s, b_s,
           conv_w, w_out, final_norm_w):
    b, seq, d = x.shape
    depth = norm_w.shape[0]
    tables = _rotary_tables(seq)
    x2 = x.reshape(b * seq, d)
    fnw = final_norm_w.reshape(1, d)
    for l in range(depth):
        lam_init = 0.8 - 0.6 * math.exp(-0.3 * l)
        w = w_in[l]
        w_bf16 = jnp.concatenate(
            [_permute_head_cols(w[:, :A_WIDTH]), _permute_head_cols(w[:, A_WIDTH:2 * A_WIDTH]),
             w[:, 2 * A_WIDTH:]], axis=1).astype(BF16)
        proj2 = _project(x2, norm_w[l].reshape(1, d), w_bf16, tables, seq)
        lamv = jnp.stack([lam_q1[l], lam_k1[l], lam_q2[l], lam_k2[l]])
        ya = _attention(proj2.reshape(b, seq, PROJ_COLS), lamv, subln_w[l].reshape(1, HEAD_W), lam_init)
        bs_b = jnp.broadcast_to(b_s[l][:, :, None], (B_GROUPS, B_CHUNK, LANES))
        x2 = _mix_out(ya.reshape(b * seq, A_WIDTH), proj2, x2,
                      sgu_ln_g[l].reshape(1, B_WIDTH), sgu_ln_b[l].reshape(1, B_WIDTH),
                      w_s[l], bs_b, conv_w[l], w_out[l].astype(BF16), fnw, seq, l == depth - 1)
    return x2.reshape(b, seq, d)
```

```python
import functools
import math

import jax
import jax.numpy as jnp
from jax import lax
from jax.experimental import pallas as pl
from jax.experimental.pallas import tpu as pltpu

F32 = jnp.float32
BF16 = jnp.bfloat16

D_MODEL = 2048
A_HEADS = 8
QK_DIM = 64
HEAD_W = 2 * QK_DIM
A_WIDTH = A_HEADS * HEAD_W
B_GROUPS = 4
B_CHUNK = 128
B_WIDTH = 512
C_WIDTH = 512
CONV_K = 3
PROJ_COLS = 4 * A_WIDTH + 3 * B_WIDTH + 4 * C_WIDTH
ROPE_THETA = 10000.0
NORM_EPS = 1e-5
LN_EPS = 1e-5
LOG2E = 1.4426950408889634

LANES = 128
VMEM_LIMIT = 56 * 1024 * 1024

PROJ_TM = 1024
PROJ_TN = 512
NORM_ROWS = 128
ATT_T = 512
MIX_TM = 512
HALO = 8

NEG_BIG = -1e30


def _silu(g):
    return g * jax.nn.sigmoid(g)


def _proj_kernel(x_ref, nw_ref, w_ref, tab_ref, o_ref, h_scr, *, n_qk_tiles):
    j = pl.program_id(1)

    @pl.when(j == 0)
    def _():
        nw = nw_ref[...]

        def norm_rows(r, carry):
            rows = pl.ds(pl.multiple_of(r * NORM_ROWS, NORM_ROWS), NORM_ROWS)
            x = x_ref[rows, :]
            ms = jnp.mean(x * x, axis=-1, keepdims=True)
            h_scr[rows, :] = (x * lax.rsqrt(ms + NORM_EPS) * nw).astype(BF16)
            return carry

        lax.fori_loop(0, PROJ_TM // NORM_ROWS, norm_rows, 0)

    acc = jnp.dot(h_scr[...], w_ref[...], preferred_element_type=F32)

    @pl.when(j < n_qk_tiles)
    def _():
        cos = tab_ref[0, 0]
        sin = tab_ref[0, 1]
        for c in range(PROJ_TN // LANES):
            cols = slice(c * LANES, (c + 1) * LANES)
            a = acc[:, cols]
            o_ref[:, cols] = (a * cos + pltpu.roll(a, LANES // 2, 1) * sin).astype(BF16)

    @pl.when(j >= n_qk_tiles)
    def _():
        o_ref[...] = acc.astype(BF16)


def _project(x2, norm_w, w_bf16, tables, seq):
    n_rows = x2.shape[0]
    n_qk_tiles = (2 * A_WIDTH) // PROJ_TN
    pos_blocks = seq // PROJ_TM

    def tab_map(i, j):
        return (jnp.minimum(j // (n_qk_tiles // 2), 1), 0, i % pos_blocks, 0)

    return pl.pallas_call(
        functools.partial(_proj_kernel, n_qk_tiles=n_qk_tiles),
        out_shape=jax.ShapeDtypeStruct((n_rows, PROJ_COLS), BF16),
        grid=(n_rows // PROJ_TM, PROJ_COLS // PROJ_TN),
        in_specs=[
            pl.BlockSpec((PROJ_TM, D_MODEL), lambda i, j: (i, 0)),
            pl.BlockSpec((1, D_MODEL), lambda i, j: (0, 0)),
            pl.BlockSpec((D_MODEL, PROJ_TN), lambda i, j: (0, j)),
            pl.BlockSpec((1, 2, PROJ_TM, LANES), tab_map),
        ],
        out_specs=pl.BlockSpec((PROJ_TM, PROJ_TN), lambda i, j: (i, j)),
        scratch_shapes=[pltpu.VMEM((PROJ_TM, D_MODEL), BF16)],
        compiler_params=pltpu.CompilerParams(
            dimension_semantics=("arbitrary", "arbitrary"), vmem_limit_bytes=VMEM_LIMIT),
        name="proj",
    )(x2, norm_w, w_bf16, tables)


def _attn_kernel(lamv_ref, sw_ref, q_ref, k_ref, v_ref, g_ref, o_ref, vt_scr, qm_scr, s_scr, acc_scr,
                 *, lam_init, seq):
    t = ATT_T
    qi = pl.program_id(2)

    @pl.when(qi == 0)
    def _():
        for c in range(seq // t):
            vt_scr[c] = v_ref[0, c * t:(c + 1) * t, :].astype(F32).T.astype(BF16)

    lane = lax.broadcasted_iota(jnp.int32, (t, HEAD_W), 1)
    is_c1 = (lane % (2 * 32)) < 32
    qf = q_ref[0].astype(F32)
    qm_scr[0] = jnp.where(is_c1, qf, 0.0).astype(BF16)
    qm_scr[1] = jnp.where(is_c1, 0.0, qf).astype(BF16)
    acc_scr[...] = jnp.zeros_like(acc_scr)

    def produce(kt, buf):
        kb = k_ref[0, pl.ds(pl.multiple_of(kt * t, t), t), :]
        bmax = []
        for c in range(2):
            s = lax.dot_general(kb, qm_scr[c], (((1,), (1,)), ((), ())), preferred_element_type=F32)
            s_scr[buf, c] = s
            bmax.append(jnp.max(s, axis=0, keepdims=True))
        return tuple(bmax)

    def consume(kt, buf, stats, bmax, masked):
        vtb = vt_scr[kt]
        new = []
        for c in range(2):
            m_old, l_old = stats[2 * c], stats[2 * c + 1]
            s = s_scr[buf, c]
            if masked:
                row = lax.broadcasted_iota(jnp.int32, (t, t), 0)
                col = lax.broadcasted_iota(jnp.int32, (t, t), 1)
                s = jnp.where(row <= col, s, NEG_BIG)
                bm = jnp.max(s, axis=0, keepdims=True)
            else:
                bm = bmax[c]
            m_new = jnp.maximum(m_old, bm)
            alpha = jnp.exp2(m_old - m_new)
            p = jnp.exp2(s - m_new)
            l_new = alpha * l_old + jnp.sum(p, axis=0, keepdims=True)
            acc_scr[c] = alpha * acc_scr[c] + jnp.dot(vtb, p.astype(BF16), preferred_element_type=F32)
            new += [m_new, l_new]
        return tuple(new)

    def pair(j, carry):
        stats, bm_a = carry[:4], carry[4:]
        bm_b = produce(2 * j + 1, 1)
        stats = consume(2 * j, 0, stats, bm_a, False)
        bm_a = produce(2 * j + 2, 0)
        stats = consume(2 * j + 1, 1, stats, bm_b, False)
        return stats + bm_a

    stats0 = (jnp.full((1, t), NEG_BIG, F32), jnp.zeros((1, t), F32)) * 2
    carry = lax.fori_loop(0, lax.shift_right_logical(qi, 1), pair, stats0 + produce(0, 0))

    def odd_tail(carry):
        produce(qi, 1)
        stats = consume(qi - 1, 0, carry[:4], carry[4:], False)
        return consume(qi, 1, stats, None, True)

    def even_tail(carry):
        return consume(qi, 0, carry[:4], None, True)

    _, l1, _, l2 = lax.cond((qi & 1) == 1, odd_tail, even_tail, carry)

    lamv = lamv_ref[...]
    lam = (jnp.exp(jnp.sum(lamv[0:1] * lamv[1:2], axis=-1, keepdims=True))
           - jnp.exp(jnp.sum(lamv[2:3] * lamv[3:4], axis=-1, keepdims=True)) + lam_init)
    d = acc_scr[0] * (1.0 / l1) - lam * (acc_scr[1] * (1.0 / l2))
    ms = jnp.mean(d * d, axis=0, keepdims=True)
    y = (d * lax.rsqrt(ms + NORM_EPS)).T
    g = g_ref[0].astype(F32)
    o_ref[0] = (y * (sw_ref[...] * (1.0 - lam_init)) * _silu(g)).astype(BF16)


def _attention(proj3, lamv, subln_w, lam_init):
    b, seq, _ = proj3.shape
    t = ATT_T
    blk = lambda off, full: pl.BlockSpec(
        (1, seq if full else t, HEAD_W),
        (lambda bi, h, qi: (bi, 0, off + h)) if full else (lambda bi, h, qi: (bi, qi, off + h)))
    return pl.pallas_call(
        functools.partial(_attn_kernel, lam_init=lam_init, seq=seq),
        out_shape=jax.ShapeDtypeStruct((b, seq, A_WIDTH), BF16),
        grid=(b, A_HEADS, seq // t),
        in_specs=[
            pl.BlockSpec((4, QK_DIM), lambda bi, h, qi: (0, 0)),
            pl.BlockSpec((1, HEAD_W), lambda bi, h, qi: (0, 0)),
            blk(0, False),
            blk(A_HEADS, True),
            blk(2 * A_HEADS, True),
            blk(3 * A_HEADS, False),
        ],
        out_specs=pl.BlockSpec((1, t, HEAD_W), lambda bi, h, qi: (bi, qi, h)),
        scratch_shapes=[
            pltpu.VMEM((seq // t, HEAD_W, t), BF16),
            pltpu.VMEM((2, t, HEAD_W), BF16),
            pltpu.VMEM((2, 2, t, t), F32),
            pltpu.VMEM((2, HEAD_W, t), F32),
        ],
        compiler_params=pltpu.CompilerParams(
            dimension_semantics=("arbitrary", "arbitrary", "arbitrary"), vmem_limit_bytes=VMEM_LIMIT),
        name="diff_attn",
    )(lamv, subln_w, proj3, proj3, proj3, proj3)


def _mix_kernel(ya_ref, u_ref, vs_ref, gb_ref, xc_ref, bg_ref, cg_ref, gc_ref, xcp_ref, cgp_ref,
                lng_ref, lnb_ref, ws_ref, bs_ref, cw_ref, wout_ref, x_ref, fnw_ref, o_ref,
                cat_scr, z_scr, *, is_last, tiles_per_seq):
    tm = MIX_TM
    i = pl.program_id(0)

    v = vs_ref[...].astype(F32)
    mu = jnp.mean(v, axis=-1, keepdims=True)
    vc = v - mu
    var = jnp.mean(vc * vc, axis=-1, keepdims=True)
    vn = (vc * lax.rsqrt(var + LN_EPS) * lng_ref[...] + lnb_ref[...]).astype(BF16)
    row = lax.broadcasted_iota(jnp.int32, (B_CHUNK, B_CHUNK), 0)
    col = lax.broadcasted_iota(jnp.int32, (B_CHUNK, B_CHUNK), 1)
    for g in range(B_GROUPS):
        cols = slice(g * LANES, (g + 1) * LANES)
        w = jnp.where(row >= col, ws_ref[g], 0.0).astype(BF16)
        bias = bs_ref[g]
        for c in range(tm // B_CHUNK):
            rows = slice(c * B_CHUNK, (c + 1) * B_CHUNK)
            mixed = jnp.dot(w, vn[rows, cols], preferred_element_type=F32) + bias
            yb = u_ref[rows, cols].astype(F32) * mixed * _silu(gb_ref[rows, cols].astype(F32))
            cat_scr[rows, cols] = yb.astype(BF16)

    seq_start = (i % tiles_per_seq) == 0
    zp = cgp_ref[...].astype(F32) * xcp_ref[...].astype(F32)
    z_scr[0:HALO, :] = jnp.where(seq_start, 0.0, zp)
    z = cg_ref[...].astype(F32) * xc_ref[...].astype(F32)
    z_scr[HALO:HALO + tm, :] = z
    cw = cw_ref[...]
    conv = cw[2:3] * z
    for tap in range(CONV_K - 1):
        shift = CONV_K - 1 - tap
        conv = conv + cw[tap:tap + 1] * z_scr[HALO - shift:HALO - shift + tm, :]
    yc = bg_ref[...].astype(F32) * conv * _silu(gc_ref[...].astype(F32))
    cat_scr[:, B_WIDTH:B_WIDTH + C_WIDTH] = yc.astype(BF16)

    y = jnp.dot(ya_ref[...], wout_ref[0:A_WIDTH, :], preferred_element_type=F32)
    y = y + jnp.dot(cat_scr[...], wout_ref[A_WIDTH:, :], preferred_element_type=F32)
    out = x_ref[...] + y
    if is_last:
        ms = jnp.mean(out * out, axis=-1, keepdims=True)
        out = out * lax.rsqrt(ms + NORM_EPS) * fnw_ref[...]
    o_ref[...] = out


def _mix_out(ya2, proj2, x2, ln_g, ln_b, w_s, bs_b, conv_w, wout_bf16, fnw, seq, is_last):
    n_rows = x2.shape[0]
    tm = MIX_TM
    base = (4 * A_WIDTH) // B_WIDTH
    pblk = lambda k: pl.BlockSpec((tm, B_WIDTH), lambda i: (i, base + k))
    halo = lambda k: pl.BlockSpec((HALO, B_WIDTH), lambda i: (jnp.maximum(i * (tm // HALO) - 1, 0), base + k))
    const = lambda shape: pl.BlockSpec(shape, lambda i: (0,) * len(shape))
    return pl.pallas_call(
        functools.partial(_mix_kernel, is_last=is_last, tiles_per_seq=seq // tm),
        out_shape=jax.ShapeDtypeStruct((n_rows, D_MODEL), F32),
        grid=(n_rows // tm,),
        in_specs=[
            pl.BlockSpec((tm, A_WIDTH), lambda i: (i, 0)),
            pblk(0), pblk(1), pblk(2),
            pblk(3), pblk(4), pblk(5), pblk(6),
            halo(3), halo(5),
            const((1, B_WIDTH)), const((1, B_WIDTH)),
            const((B_GROUPS, B_CHUNK, B_CHUNK)),
            const((B_GROUPS, B_CHUNK, LANES)),
            const((CONV_K, C_WIDTH)),
            const((D_MODEL, D_MODEL)),
            pl.BlockSpec((tm, D_MODEL), lambda i: (i, 0)),
            const((1, D_MODEL)),
        ],
        out_specs=pl.BlockSpec((tm, D_MODEL), lambda i: (i, 0)),
        scratch_shapes=[
            pltpu.VMEM((tm, B_WIDTH + C_WIDTH), BF16),
            pltpu.VMEM((HALO + tm, C_WIDTH), F32),
        ],
        compiler_params=pltpu.CompilerParams(
            dimension_semantics=("arbitrary",), vmem_limit_bytes=VMEM_LIMIT),
        name="mix_out",
    )(ya2, proj2, proj2, proj2, proj2, proj2, proj2, proj2, proj2, proj2,
      ln_g, ln_b, w_s, bs_b, conv_w, wout_bf16, x2, fnw)


def _permute_head_cols(w):
    d = w.shape[0]
    return w.reshape(d, A_HEADS, 2, 2, QK_DIM // 2).transpose(0, 1, 3, 2, 4).reshape(d, A_WIDTH)


def _rotary_tables(seq):
    pos = jnp.arange(seq, dtype=F32)
    inv_freq = ROPE_THETA ** (-jnp.arange(0, QK_DIM, 2, dtype=F32) / QK_DIM)
    ang = pos[:, None] * inv_freq[None, :]
    cos = jnp.tile(jnp.cos(ang), (1, LANES // (QK_DIM // 2)))
    sin = jnp.tile(jnp.sin(ang), (1, LANES // (QK_DIM // 2)))
    sign = jnp.where(jnp.arange(LANES) < LANES // 2, -1.0, 1.0).astype(F32)
    k_tab = jnp.stack([cos, sin * sign])
    q_tab = k_tab * (QK_DIM ** -0.5 * LOG2E)
    return jnp.stack([q_tab, k_tab])


def kernel(x, norm_w, w_in, lam_q1, lam_k1, lam_q2, lam_k2, subln_w, sgu_ln_g, sgu_ln_b, w_s, b_s,
           conv_w, w_out, final_norm_w):
    b, seq, d = x.shape
    depth = norm_w.shape[0]
    tables = _rotary_tables(seq)
    x2 = x.reshape(b * seq, d)
    fnw = final_norm_w.reshape(1, d)
    for l in range(depth):
        lam_init = 0.8 - 0.6 * math.exp(-0.3 * l)
        w = w_in[l]
        w_bf16 = jnp.concatenate(
            [_permute_head_cols(w[:, :A_WIDTH]), _permute_head_cols(w[:, A_WIDTH:2 * A_WIDTH]),
             w[:, 2 * A_WIDTH:]], axis=1).astype(BF16)
        proj2 = _project(x2, norm_w[l].reshape(1, d), w_bf16, tables, seq)
        lamv = jnp.stack([lam_q1[l], lam_k1[l], lam_q2[l], lam_k2[l]])
        ya = _attention(proj2.reshape(b, seq, PROJ_COLS), lamv, subln_w[l].reshape(1, HEAD_W), lam_init)
        bs_b = jnp.broadcast_to(b_s[l][:, :, None], (B_GROUPS, B_CHUNK, LANES))
        x2 = _mix_out(ya.reshape(b * seq, A_WIDTH), proj2, x2,
                      sgu_ln_g[l].reshape(1, B_WIDTH), sgu_ln_b[l].reshape(1, B_WIDTH),
                      w_s[l], bs_b, conv_w[l], w_out[l].astype(BF16), fnw, seq, l == depth - 1)
    return x2.reshape(b, seq, d)
```

```python
import functools
import math

import jax
import jax.numpy as jnp
from jax import lax
from jax.experimental import pallas as pl
from jax.experimental.pallas import tpu as pltpu

F32 = jnp.float32
BF16 = jnp.bfloat16

D_MODEL = 2048
A_HEADS = 8
QK_DIM = 64
HEAD_W = 2 * QK_DIM
A_WIDTH = A_HEADS * HEAD_W
B_GROUPS = 4
B_CHUNK = 128
B_WIDTH = 512
C_WIDTH = 512
CONV_K = 3
PROJ_COLS = 4 * A_WIDTH + 3 * B_WIDTH + 4 * C_WIDTH
ROPE_THETA = 10000.0
NORM_EPS = 1e-5
LN_EPS = 1e-5
LOG2E = 1.4426950408889634

LANES = 128
VMEM_LIMIT = 56 * 1024 * 1024

PROJ_TM = 1024
PROJ_TN = 512
NORM_ROWS = 128
ATT_T = 512
MIX_TM = 512
HALO = 8

NEG_BIG = -1e30


def _silu(g):
    return g * jax.nn.sigmoid(g)


def _proj_kernel(x_ref, nw_ref, w_ref, o_ref, h_scr):
    j = pl.program_id(1)

    @pl.when(j == 0)
    def _():
        nw = nw_ref[...]

        def norm_rows(r, carry):
            rows = pl.ds(pl.multiple_of(r * NORM_ROWS, NORM_ROWS), NORM_ROWS)
            x = x_ref[rows, :]
            ms = jnp.mean(x * x, axis=-1, keepdims=True)
            h_scr[rows, :] = (x * lax.rsqrt(ms + NORM_EPS) * nw).astype(BF16)
            return carry

        lax.fori_loop(0, PROJ_TM // NORM_ROWS, norm_rows, 0)

    o_ref[...] = jnp.dot(h_scr[...], w_ref[...], preferred_element_type=F32).astype(BF16)


def _project(x2, norm_w, w_bf16):
    n_rows = x2.shape[0]
    return pl.pallas_call(
        _proj_kernel,
        out_shape=jax.ShapeDtypeStruct((n_rows, PROJ_COLS), BF16),
        grid=(n_rows // PROJ_TM, PROJ_COLS // PROJ_TN),
        in_specs=[
            pl.BlockSpec((PROJ_TM, D_MODEL), lambda i, j: (i, 0)),
            pl.BlockSpec((1, D_MODEL), lambda i, j: (0, 0)),
            pl.BlockSpec((D_MODEL, PROJ_TN), lambda i, j: (0, j)),
        ],
        out_specs=pl.BlockSpec((PROJ_TM, PROJ_TN), lambda i, j: (i, j)),
        scratch_shapes=[pltpu.VMEM((PROJ_TM, D_MODEL), BF16)],
        compiler_params=pltpu.CompilerParams(
            dimension_semantics=("arbitrary", "arbitrary"), vmem_limit_bytes=VMEM_LIMIT),
        name="proj",
    )(x2, norm_w, w_bf16)


def _rotary(x, cos, sin):
    return x * cos + pltpu.roll(x, LANES // 2, 1) * sin


def _attn_kernel(lamv_ref, sw_ref, tab_ref, q_ref, k_ref, v_ref, g_ref, o_ref,
                 kr_scr, vt_scr, qm_scr, s_scr, acc_scr, *, lam_init, seq):
    t = ATT_T
    qi = pl.program_id(2)

    @pl.when(qi == 0)
    def _():
        for c in range(seq // t):
            rows = slice(c * t, (c + 1) * t)
            vt_scr[c] = v_ref[0, rows, :].astype(F32).T.astype(BF16)
            kr_scr[rows, :] = _rotary(k_ref[0, rows, :].astype(F32), tab_ref[0, rows, :],
                                      tab_ref[1, rows, :]).astype(BF16)

    q_rows = pl.ds(pl.multiple_of(qi * t, t), t)
    qf = _rotary(q_ref[0].astype(F32), tab_ref[0, q_rows, :], tab_ref[1, q_rows, :]) * (QK_DIM ** -0.5 * LOG2E)
    lane = lax.broadcasted_iota(jnp.int32, (t, HEAD_W), 1)
    is_c1 = (lane % (2 * 32)) < 32
    qm_scr[0] = jnp.where(is_c1, qf, 0.0).astype(BF16)
    qm_scr[1] = jnp.where(is_c1, 0.0, qf).astype(BF16)
    acc_scr[...] = jnp.zeros_like(acc_scr)

    def produce(kt, buf):
        kb = kr_scr[pl.ds(pl.multiple_of(kt * t, t), t), :]
        bmax = []
        for c in range(2):
            s = lax.dot_general(kb, qm_scr[c], (((1,), (1,)), ((), ())), preferred_element_type=F32)
            s_scr[buf, c] = s
            bmax.append(jnp.max(s, axis=0, keepdims=True))
        return tuple(bmax)

    def consume(kt, buf, stats, bmax, masked):
        vtb = vt_scr[kt]
        new = []
        for c in range(2):
            m_old, l_old = stats[2 * c], stats[2 * c + 1]
            s = s_scr[buf, c]
            if masked:
                row = lax.broadcasted_iota(jnp.int32, (t, t), 0)
                col = lax.broadcasted_iota(jnp.int32, (t, t), 1)
                s = jnp.where(row <= col, s, NEG_BIG)
                bm = jnp.max(s, axis=0, keepdims=True)
            else:
                bm = bmax[c]
            m_new = jnp.maximum(m_old, bm)
            alpha = jnp.exp2(m_old - m_new)
            p = jnp.exp2(s - m_new)
            l_new = alpha * l_old + jnp.sum(p, axis=0, keepdims=True)
            acc_scr[c] = alpha * acc_scr[c] + jnp.dot(vtb, p.astype(BF16), preferred_element_type=F32)
            new += [m_new, l_new]
        return tuple(new)

    def pair(j, carry):
        stats, bm_a = carry[:4], carry[4:]
        bm_b = produce(2 * j + 1, 1)
        stats = consume(2 * j, 0, stats, bm_a, False)
        bm_a = produce(2 * j + 2, 0)
        stats = consume(2 * j + 1, 1, stats, bm_b, False)
        return stats + bm_a

    stats0 = (jnp.full((1, t), NEG_BIG, F32), jnp.zeros((1, t), F32)) * 2
    carry = lax.fori_loop(0, lax.shift_right_logical(qi, 1), pair, stats0 + produce(0, 0))

    def odd_tail(carry):
        produce(qi, 1)
        stats = consume(qi - 1, 0, carry[:4], carry[4:], False)
        return consume(qi, 1, stats, None, True)

    def even_tail(carry):
        return consume(qi, 0, carry[:4], None, True)

    _, l1, _, l2 = lax.cond((qi & 1) == 1, odd_tail, even_tail, carry)

    lamv = lamv_ref[...]
    lam = (jnp.exp(jnp.sum(lamv[0:1] * lamv[1:2], axis=-1, keepdims=True))
           - jnp.exp(jnp.sum(lamv[2:3] * lamv[3:4], axis=-1, keepdims=True)) + lam_init)
    d = acc_scr[0] * (1.0 / l1) - lam * (acc_scr[1] * (1.0 / l2))
    ms = jnp.mean(d * d, axis=0, keepdims=True)
    y = (d * lax.rsqrt(ms + NORM_EPS)).T
    g = g_ref[0].astype(F32)
    o_ref[0] = (y * (sw_ref[...] * (1.0 - lam_init)) * _silu(g)).astype(BF16)


def _attention(proj3, lamv, subln_w, tables, lam_init):
    b, seq, _ = proj3.shape
    t = ATT_T
    blk = lambda off, full: pl.BlockSpec(
        (1, seq if full else t, HEAD_W),
        (lambda bi, h, qi: (bi, 0, off + h)) if full else (lambda bi, h, qi: (bi, qi, off + h)))
    return pl.pallas_call(
        functools.partial(_attn_kernel, lam_init=lam_init, seq=seq),
        out_shape=jax.ShapeDtypeStruct((b, seq, A_WIDTH), BF16),
        grid=(b, A_HEADS, seq // t),
        in_specs=[
            pl.BlockSpec((4, QK_DIM), lambda bi, h, qi: (0, 0)),
            pl.BlockSpec((1, HEAD_W), lambda bi, h, qi: (0, 0)),
            pl.BlockSpec((2, seq, LANES), lambda bi, h, qi: (0, 0, 0)),
            blk(0, False),
            blk(A_HEADS, True),
            blk(2 * A_HEADS, True),
            blk(3 * A_HEADS, False),
        ],
        out_specs=pl.BlockSpec((1, t, HEAD_W), lambda bi, h, qi: (bi, qi, h)),
        scratch_shapes=[
            pltpu.VMEM((seq, HEAD_W), BF16),
            pltpu.VMEM((seq // t, HEAD_W, t), BF16),
            pltpu.VMEM((2, t, HEAD_W), BF16),
            pltpu.VMEM((2, 2, t, t), F32),
            pltpu.VMEM((2, HEAD_W, t), F32),
        ],
        compiler_params=pltpu.CompilerParams(
            dimension_semantics=("arbitrary", "arbitrary", "arbitrary"), vmem_limit_bytes=VMEM_LIMIT),
        name="diff_attn",
    )(lamv, subln_w, tables, proj3, proj3, proj3, proj3)


def _mix_kernel(ya_ref, u_ref, vs_ref, gb_ref, xc_ref, bg_ref, cg_ref, gc_ref, xcp_ref, cgp_ref,
                lng_ref, lnb_ref, ws_ref, bs_ref, cw_ref, wout_ref, x_ref, fnw_ref, o_ref,
                cat_scr, z_scr, *, is_last, tiles_per_seq):
    tm = MIX_TM
    i = pl.program_id(0)

    v = vs_ref[...].astype(F32)
    mu = jnp.mean(v, axis=-1, keepdims=True)
    vc = v - mu
    var = jnp.mean(vc * vc, axis=-1, keepdims=True)
    vn = (vc * lax.rsqrt(var + LN_EPS) * lng_ref[...] + lnb_ref[...]).astype(BF16)
    row = lax.broadcasted_iota(jnp.int32, (B_CHUNK, B_CHUNK), 0)
    col = lax.broadcasted_iota(jnp.int32, (B_CHUNK, B_CHUNK), 1)
    for g in range(B_GROUPS):
        cols = slice(g * LANES, (g + 1) * LANES)
        w = jnp.where(row >= col, ws_ref[g], 0.0).astype(BF16)
        bias = bs_ref[g]
        for c in range(tm // B_CHUNK):
            rows = slice(c * B_CHUNK, (c + 1) * B_CHUNK)
            mixed = jnp.dot(w, vn[rows, cols], preferred_element_type=F32) + bias
            yb = u_ref[rows, cols].astype(F32) * mixed * _silu(gb_ref[rows, cols].astype(F32))
            cat_scr[rows, cols] = yb.astype(BF16)

    seq_start = (i % tiles_per_seq) == 0
    zp = cgp_ref[...].astype(F32) * xcp_ref[...].astype(F32)
    z_scr[0:HALO, :] = jnp.where(seq_start, 0.0, zp)
    z = cg_ref[...].astype(F32) * xc_ref[...].astype(F32)
    z_scr[HALO:HALO + tm, :] = z
    cw = cw_ref[...]
    conv = cw[2:3] * z
    for tap in range(CONV_K - 1):
        shift = CONV_K - 1 - tap
        conv = conv + cw[tap:tap + 1] * z_scr[HALO - shift:HALO - shift + tm, :]
    yc = bg_ref[...].astype(F32) * conv * _silu(gc_ref[...].astype(F32))
    cat_scr[:, B_WIDTH:B_WIDTH + C_WIDTH] = yc.astype(BF16)

    y = jnp.dot(ya_ref[...], wout_ref[0:A_WIDTH, :], preferred_element_type=F32)
    y = y + jnp.dot(cat_scr[...], wout_ref[A_WIDTH:, :], preferred_element_type=F32)
    out = x_ref[...] + y
    if is_last:
        ms = jnp.mean(out * out, axis=-1, keepdims=True)
        out = out * lax.rsqrt(ms + NORM_EPS) * fnw_ref[...]
    o_ref[...] = out


def _mix_out(ya2, proj2, x2, ln_g, ln_b, w_s, bs_b, conv_w, wout_bf16, fnw, seq, is_last):
    n_rows = x2.shape[0]
    tm = MIX_TM
    base = (4 * A_WIDTH) // B_WIDTH
    pblk = lambda k: pl.BlockSpec((tm, B_WIDTH), lambda i: (i, base + k))
    halo = lambda k: pl.BlockSpec((HALO, B_WIDTH), lambda i: (jnp.maximum(i * (tm // HALO) - 1, 0), base + k))
    const = lambda shape: pl.BlockSpec(shape, lambda i: (0,) * len(shape))
    return pl.pallas_call(
        functools.partial(_mix_kernel, is_last=is_last, tiles_per_seq=seq // tm),
        out_shape=jax.ShapeDtypeStruct((n_rows, D_MODEL), F32),
        grid=(n_rows // tm,),
        in_specs=[
            pl.BlockSpec((tm, A_WIDTH), lambda i: (i, 0)),
            pblk(0), pblk(1), pblk(2),
            pblk(3), pblk(4), pblk(5), pblk(6),
            halo(3), halo(5),
            const((1, B_WIDTH)), const((1, B_WIDTH)),
            const((B_GROUPS, B_CHUNK, B_CHUNK)),
            const((B_GROUPS, B_CHUNK, LANES)),
            const((CONV_K, C_WIDTH)),
            const((D_MODEL, D_MODEL)),
            pl.BlockSpec((tm, D_MODEL), lambda i: (i, 0)),
            const((1, D_MODEL)),
        ],
        out_specs=pl.BlockSpec((tm, D_MODEL), lambda i: (i, 0)),
        scratch_shapes=[
            pltpu.VMEM((tm, B_WIDTH + C_WIDTH), BF16),
            pltpu.VMEM((HALO + tm, C_WIDTH), F32),
        ],
        compiler_params=pltpu.CompilerParams(
            dimension_semantics=("arbitrary",), vmem_limit_bytes=VMEM_LIMIT),
        name="mix_out",
    )(ya2, proj2, proj2, proj2, proj2, proj2, proj2, proj2, proj2, proj2,
      ln_g, ln_b, w_s, bs_b, conv_w, wout_bf16, x2, fnw)


def _permute_head_cols(w):
    d = w.shape[0]
    return w.reshape(d, A_HEADS, 2, 2, QK_DIM // 2).transpose(0, 1, 3, 2, 4).reshape(d, A_WIDTH)


def _rotary_tables(seq):
    pos = jnp.arange(seq, dtype=F32)
    inv_freq = ROPE_THETA ** (-jnp.arange(0, QK_DIM, 2, dtype=F32) / QK_DIM)
    ang = pos[:, None] * inv_freq[None, :]
    cos = jnp.tile(jnp.cos(ang), (1, LANES // (QK_DIM // 2)))
    sin = jnp.tile(jnp.sin(ang), (1, LANES // (QK_DIM // 2)))
    sign = jnp.where(jnp.arange(LANES) < LANES // 2, -1.0, 1.0).astype(F32)
    return jnp.stack([cos, sin * sign])


def kernel(x, norm_w, w_in, lam_q1, lam_k1, lam_q2, lam_k2, subln_w, sgu_ln_g, sgu_ln_b, w_s, b_s,
           conv_w, w_out, final_norm_w):
    b, seq, d = x.shape
    depth = norm_w.shape[0]
    tables = _rotary_tables(seq)
    x2 = x.reshape(b * seq, d)
    fnw = final_norm_w.reshape(1, d)
    for l in range(depth):
        lam_init = 0.8 - 0.6 * math.exp(-0.3 * l)
        w = w_in[l]
        w_bf16 = jnp.concatenate(
            [_permute_head_cols(w[:, :A_WIDTH]), _permute_head_cols(w[:, A_WIDTH:2 * A_WIDTH]),
             w[:, 2 * A_WIDTH:]], axis=1).astype(BF16)
        proj2 = _project(x2, norm_w[l].reshape(1, d), w_bf16)
        lamv = jnp.stack([lam_q1[l], lam_k1[l], lam_q2[l], lam_k2[l]])
        ya = _attention(proj2.reshape(b, seq, PROJ_COLS), lamv, subln_w[l].reshape(1, HEAD_W), tables,
                        lam_init)
        bs_b = jnp.broadcast_to(b_s[l][:, :, None], (B_GROUPS, B_CHUNK, LANES))
        x2 = _mix_out(ya.reshape(b * seq, A_WIDTH), proj2, x2,
                      sgu_ln_g[l].reshape(1, B_WIDTH), sgu_ln_b[l].reshape(1, B_WIDTH),
                      w_s[l], bs_b, conv_w[l], w_out[l].astype(BF16), fnw, seq, l == depth - 1)
    return x2.reshape(b, seq, d)
```

```python
import functools
import math

import jax
import jax.numpy as jnp
from jax import lax
from jax.experimental import pallas as pl
from jax.experimental.pallas import tpu as pltpu

F32 = jnp.float32
BF16 = jnp.bfloat16

D_MODEL = 2048
A_HEADS = 8
QK_DIM = 64
HEAD_W = 2 * QK_DIM
A_WIDTH = A_HEADS * HEAD_W
B_GROUPS = 4
B_CHUNK = 128
B_WIDTH = 512
C_WIDTH = 512
CONV_K = 3
PROJ_COLS = 4 * A_WIDTH + 3 * B_WIDTH + 4 * C_WIDTH
ROPE_THETA = 10000.0
NORM_EPS = 1e-5
LN_EPS = 1e-5
LOG2E = 1.4426950408889634

LANES = 128
VMEM_LIMIT = 56 * 1024 * 1024

PROJ_TM = 1024
PROJ_TN = 512
NORM_ROWS = 128
ATT_TK = 512
ATT_NSUB = 2
ATT_TQ = ATT_NSUB * ATT_TK
MIX_TM = 512
HALO = 8

NEG_BIG = -1e30


def _silu(g):
    return g * jax.nn.sigmoid(g)


def _proj_kernel(x_ref, nw_ref, w_ref, o_ref, h_scr):
    j = pl.program_id(1)

    @pl.when(j == 0)
    def _():
        nw = nw_ref[...]

        def norm_rows(r, carry):
            rows = pl.ds(pl.multiple_of(r * NORM_ROWS, NORM_ROWS), NORM_ROWS)
            x = x_ref[rows, :]
            ms = jnp.mean(x * x, axis=-1, keepdims=True)
            h_scr[rows, :] = (x * lax.rsqrt(ms + NORM_EPS) * nw).astype(BF16)
            return carry

        lax.fori_loop(0, PROJ_TM // NORM_ROWS, norm_rows, 0)

    o_ref[...] = jnp.dot(h_scr[...], w_ref[...].astype(BF16), preferred_element_type=F32).astype(BF16)


def _project(x2, norm_w, w_in, layer):
    n_rows = x2.shape[0]
    return pl.pallas_call(
        _proj_kernel,
        out_shape=jax.ShapeDtypeStruct((n_rows, PROJ_COLS), BF16),
        grid=(n_rows // PROJ_TM, PROJ_COLS // PROJ_TN),
        in_specs=[
            pl.BlockSpec((PROJ_TM, D_MODEL), lambda i, j: (i, 0)),
            pl.BlockSpec((None, 1, D_MODEL), lambda i, j: (layer, 0, 0)),
            pl.BlockSpec((None, D_MODEL, PROJ_TN), lambda i, j: (layer, 0, j)),
        ],
        out_specs=pl.BlockSpec((PROJ_TM, PROJ_TN), lambda i, j: (i, j)),
        scratch_shapes=[pltpu.VMEM((PROJ_TM, D_MODEL), BF16)],
        compiler_params=pltpu.CompilerParams(
            dimension_semantics=("arbitrary", "arbitrary"), vmem_limit_bytes=VMEM_LIMIT),
        name="proj",
    )(x2, norm_w, w_in)


def _rotary(x, cos, sin):
    half = QK_DIM // 2
    lane = lax.broadcasted_iota(jnp.int32, x.shape, 1)
    partner = jnp.where(lane % QK_DIM < half, pltpu.roll(x, LANES - half, 1), pltpu.roll(x, half, 1))
    return x * cos + partner * sin


def _tile_rows(tile, size):
    if isinstance(tile, int):
        return pl.ds(tile * size, size)
    return pl.ds(pl.multiple_of(tile * size, size), size)


def _attn_kernel(lamv_ref, sw_ref, tab_ref, q_ref, qn_ref, k_ref, v_ref, g_ref, o_ref,
                 kr_scr, vt_scr, qm_scr, s_scr, acc_scr, bm_scr, *, lam_init, seq):
    tk, tq, nsub = ATT_TK, ATT_TQ, ATT_NSUB
    qi = pl.program_id(2)
    n_q = seq // tq
    slot = qi & 1
    nt = (((1,), (1,)), ((), ()))

    def prep_q(src_ref, tile, dst):
        rows = _tile_rows(tile, tq)
        qf = _rotary(src_ref[0].astype(F32), tab_ref[0, rows, :], tab_ref[1, rows, :]) * (QK_DIM ** -0.5 * LOG2E)
        lane = lax.broadcasted_iota(jnp.int32, (tq, HEAD_W), 1)
        is_c1 = lane < QK_DIM
        qm_scr[dst, 0] = jnp.where(is_c1, qf, 0.0).astype(BF16)
        qm_scr[dst, 1] = jnp.where(is_c1, 0.0, qf).astype(BF16)

    def produce(kt, buf, q_slot, col0):
        kb = kr_scr[_tile_rows(kt, tk), :]
        bmax = []
        for c in range(2):
            s = lax.dot_general(kb, qm_scr[q_slot, c, col0:, :], nt, preferred_element_type=F32)
            s_scr[buf, c, :, col0:] = s
            bmax.append(jnp.max(s, axis=0, keepdims=True))
        return tuple(bmax)

    def consume(kt, buf, stats, bmax, col0, diagonal):
        vtb = vt_scr[kt]
        new = []
        for c in range(2):
            m_all, l_all = stats[2 * c], stats[2 * c + 1]
            m_old, l_old = m_all[:, col0:], l_all[:, col0:]
            s = s_scr[buf, c, :, col0:]
            bm = bmax[c]
            if diagonal:
                row = lax.broadcasted_iota(jnp.int32, (tk, tk), 0)
                col = lax.broadcasted_iota(jnp.int32, (tk, tk), 1)
                sm = jnp.where(row <= col, s[:, :tk], NEG_BIG)
                bm_m = jnp.max(sm, axis=0, keepdims=True)
                if s.shape[1] > tk:
                    s = jnp.concatenate([sm, s[:, tk:]], axis=1)
                    bm = jnp.concatenate([bm_m, bm[:, tk:]], axis=1)
                else:
                    s, bm = sm, bm_m
            m_new = jnp.maximum(m_old, bm)
            alpha = jnp.exp2(m_old - m_new)
            p = jnp.exp2(s - m_new)
            l_new = alpha * l_old + jnp.sum(p, axis=0, keepdims=True)
            acc_scr[c, :, col0:] = (alpha * acc_scr[c, :, col0:]
                                    + jnp.dot(vtb, p.astype(BF16), preferred_element_type=F32))
            if col0 > 0:
                m_new = jnp.concatenate([m_all[:, :col0], m_new], axis=1)
                l_new = jnp.concatenate([l_all[:, :col0], l_new], axis=1)
            new += [m_new, l_new]
        return tuple(new)

    @pl.when(qi == 0)
    def _():
        for c in range(seq // tk):
            rows = slice(c * tk, (c + 1) * tk)
            vt_scr[c] = v_ref[0, rows, :].astype(F32).T.astype(BF16)
            kr_scr[rows, :] = _rotary(k_ref[0, rows, :].astype(F32), tab_ref[0, rows, :],
                                      tab_ref[1, rows, :]).astype(BF16)
        prep_q(q_ref, 0, 0)
        bm0 = produce(0, 0, 0, 0)
        bm_scr[0] = bm0[0]
        bm_scr[1] = bm0[1]

    acc_scr[...] = jnp.zeros_like(acc_scr)

    def pair(j, carry):
        stats, bm_a = carry[:4], carry[4:]
        bm_b = produce(2 * j + 1, 1, slot, 0)
        stats = consume(2 * j, 0, stats, bm_a, 0, False)
        bm_a = produce(2 * j + 2, 0, slot, 0)
        stats = consume(2 * j + 1, 1, stats, bm_b, 0, False)
        return stats + bm_a

    n_full = nsub * qi
    stats0 = (jnp.full((1, tq), NEG_BIG, F32), jnp.zeros((1, tq), F32)) * 2
    carry = lax.fori_loop(0, qi * (nsub // 2), pair, stats0 + (bm_scr[0], bm_scr[1]))

    stats, bm = carry[:4], carry[4:]
    for r in range(nsub):
        bm_next = produce(n_full + r + 1, (r + 1) % 2, slot, (r + 1) * tk) if r + 1 < nsub else None
        stats = consume(n_full + r, r % 2, stats, bm, r * tk, True)
        bm = bm_next
    _, l1, _, l2 = stats

    lamv = lamv_ref[...]
    lam = (jnp.exp(jnp.sum(lamv[0:1] * lamv[1:2], axis=-1, keepdims=True))
           - jnp.exp(jnp.sum(lamv[2:3] * lamv[3:4], axis=-1, keepdims=True)) + lam_init)
    d = acc_scr[0] * (1.0 / l1) - lam * (acc_scr[1] * (1.0 / l2))
    ms = jnp.mean(d * d, axis=0, keepdims=True)
    y = (d * lax.rsqrt(ms + NORM_EPS)).T
    g = g_ref[0].astype(F32)
    o_ref[0] = (y * (sw_ref[...] * (1.0 - lam_init)) * _silu(g)).astype(BF16)

    prep_q(qn_ref, jnp.minimum(qi + 1, n_q - 1), 1 - slot)
    bm0 = produce(0, 0, 1 - slot, 0)
    bm_scr[0] = bm0[0]
    bm_scr[1] = bm0[1]


def _attention(proj3, lamv, subln_w, tables, layer, lam_init):
    b, seq, _ = proj3.shape
    tk, tq = ATT_TK, ATT_TQ
    n_q = seq // tq
    head_blk = lambda off: pl.BlockSpec((1, seq, HEAD_W), lambda bi, h, qi: (bi, 0, off + h))
    tile_blk = lambda off: pl.BlockSpec((1, tq, HEAD_W), lambda bi, h, qi: (bi, qi, off + h))
    return pl.pallas_call(
        functools.partial(_attn_kernel, lam_init=lam_init, seq=seq),
        out_shape=jax.ShapeDtypeStruct((b, seq, A_WIDTH), BF16),
        grid=(b, A_HEADS, n_q),
        in_specs=[
            pl.BlockSpec((None, 4, QK_DIM), lambda bi, h, qi: (layer, 0, 0)),
            pl.BlockSpec((None, 1, HEAD_W), lambda bi, h, qi: (layer, 0, 0)),
            pl.BlockSpec((2, seq, LANES), lambda bi, h, qi: (0, 0, 0)),
            tile_blk(0),
            pl.BlockSpec((1, tq, HEAD_W), lambda bi, h, qi: (bi, jnp.minimum(qi + 1, n_q - 1), h)),
            head_blk(A_HEADS),
            head_blk(2 * A_HEADS),
            tile_blk(3 * A_HEADS),
        ],
        out_specs=pl.BlockSpec((1, tq, HEAD_W), lambda bi, h, qi: (bi, qi, h)),
        scratch_shapes=[
            pltpu.VMEM((seq, HEAD_W), BF16),
            pltpu.VMEM((seq // tk, HEAD_W, tk), BF16),
            pltpu.VMEM((2, 2, tq, HEAD_W), BF16),
            pltpu.VMEM((2, 2, tk, tq), F32),
            pltpu.VMEM((2, HEAD_W, tq), F32),
            pltpu.VMEM((2, 1, tq), F32),
        ],
        compiler_params=pltpu.CompilerParams(
            dimension_semantics=("arbitrary", "arbitrary", "arbitrary"), vmem_limit_bytes=VMEM_LIMIT),
        name="diff_attn",
    )(lamv, subln_w, tables, proj3, proj3, proj3, proj3, proj3)


def _mix_kernel(ya_ref, u_ref, vs_ref, gb_ref, xc_ref, bg_ref, cg_ref, gc_ref, xcp_ref, cgp_ref,
                lng_ref, lnb_ref, ws_ref, bs_ref, cw_ref, wout_ref, x_ref, fnw_ref, o_ref,
                cat_scr, z_scr, *, is_last, tiles_per_seq):
    tm = MIX_TM
    i = pl.program_id(0)

    v = vs_ref[...].astype(F32)
    mu = jnp.mean(v, axis=-1, keepdims=True)
    vc = v - mu
    var = jnp.mean(vc * vc, axis=-1, keepdims=True)
    vn = (vc * lax.rsqrt(var + LN_EPS) * lng_ref[...] + lnb_ref[...]).astype(BF16)
    row = lax.broadcasted_iota(jnp.int32, (B_CHUNK, B_CHUNK), 0)
    col = lax.broadcasted_iota(jnp.int32, (B_CHUNK, B_CHUNK), 1)
    for g in range(B_GROUPS):
        cols = slice(g * LANES, (g + 1) * LANES)
        w = jnp.where(row >= col, ws_ref[g], 0.0).astype(BF16)
        bias = bs_ref[g]
        for c in range(tm // B_CHUNK):
            rows = slice(c * B_CHUNK, (c + 1) * B_CHUNK)
            mixed = jnp.dot(w, vn[rows, cols], preferred_element_type=F32) + bias
            yb = u_ref[rows, cols].astype(F32) * mixed * _silu(gb_ref[rows, cols].astype(F32))
            cat_scr[rows, cols] = yb.astype(BF16)

    seq_start = (i % tiles_per_seq) == 0
    zp = cgp_ref[...].astype(F32) * xcp_ref[...].astype(F32)
    z_scr[0:HALO, :] = jnp.where(seq_start, 0.0, zp)
    z = cg_ref[...].astype(F32) * xc_ref[...].astype(F32)
    z_scr[HALO:HALO + tm, :] = z
    cw = cw_ref[...]
    conv = cw[2:3] * z
    for tap in range(CONV_K - 1):
        shift = CONV_K - 1 - tap
        conv = conv + cw[tap:tap + 1] * z_scr[HALO - shift:HALO - shift + tm, :]
    yc = bg_ref[...].astype(F32) * conv * _silu(gc_ref[...].astype(F32))
    cat_scr[:, B_WIDTH:B_WIDTH + C_WIDTH] = yc.astype(BF16)

    y = jnp.dot(ya_ref[...], wout_ref[0:A_WIDTH, :], preferred_element_type=F32)
    y = y + jnp.dot(cat_scr[...], wout_ref[A_WIDTH:, :], preferred_element_type=F32)
    out = x_ref[...] + y
    if is_last:
        ms = jnp.mean(out * out, axis=-1, keepdims=True)
        out = out * lax.rsqrt(ms + NORM_EPS) * fnw_ref[...]
    o_ref[...] = out


def _mix_out(ya2, proj2, x2, ln_g, ln_b, w_s, bs_b, conv_w, wout_bf16, layer, fnw, seq, is_last):
    n_rows = x2.shape[0]
    tm = MIX_TM
    base = (4 * A_WIDTH) // B_WIDTH
    pblk = lambda k: pl.BlockSpec((tm, B_WIDTH), lambda i: (i, base + k))
    halo = lambda k: pl.BlockSpec((HALO, B_WIDTH), lambda i: (jnp.maximum(i * (tm // HALO) - 1, 0), base + k))
    per_layer = lambda shape: pl.BlockSpec((None,) + shape, lambda i: (layer,) + (0,) * len(shape))
    return pl.pallas_call(
        functools.partial(_mix_kernel, is_last=is_last, tiles_per_seq=seq // tm),
        out_shape=jax.ShapeDtypeStruct((n_rows, D_MODEL), F32),
        grid=(n_rows // tm,),
        in_specs=[
            pl.BlockSpec((tm, A_WIDTH), lambda i: (i, 0)),
            pblk(0), pblk(1), pblk(2),
            pblk(3), pblk(4), pblk(5), pblk(6),
            halo(3), halo(5),
            per_layer((1, B_WIDTH)), per_layer((1, B_WIDTH)),
            per_layer((B_GROUPS, B_CHUNK, B_CHUNK)),
            per_layer((B_GROUPS, B_CHUNK, LANES)),
            per_layer((CONV_K, C_WIDTH)),
            per_layer((D_MODEL, D_MODEL)),
            pl.BlockSpec((tm, D_MODEL), lambda i: (i, 0)),
            pl.BlockSpec((1, D_MODEL), lambda i: (0, 0)),
        ],
        out_specs=pl.BlockSpec((tm, D_MODEL), lambda i: (i, 0)),
        scratch_shapes=[
            pltpu.VMEM((tm, B_WIDTH + C_WIDTH), BF16),
            pltpu.VMEM((HALO + tm, C_WIDTH), F32),
        ],
        compiler_params=pltpu.CompilerParams(
            dimension_semantics=("arbitrary",), vmem_limit_bytes=VMEM_LIMIT),
        name="mix_out",
    )(ya2, proj2, proj2, proj2, proj2, proj2, proj2, proj2, proj2, proj2,
      ln_g, ln_b, w_s, bs_b, conv_w, wout_bf16, x2, fnw)


def _rotary_tables(seq):
    half = QK_DIM // 2
    pos = jnp.arange(seq, dtype=F32)
    inv_freq = ROPE_THETA ** (-jnp.arange(0, QK_DIM, 2, dtype=F32) / QK_DIM)
    ang = pos[:, None] * inv_freq[None, :]
    cos = jnp.tile(jnp.cos(ang), (1, LANES // half))
    sin = jnp.tile(jnp.sin(ang), (1, LANES // half))
    sign = jnp.where(jnp.arange(LANES) % QK_DIM < half, -1.0, 1.0).astype(F32)
    return jnp.stack([cos, sin * sign])


def kernel(x, norm_w, w_in, lam_q1, lam_k1, lam_q2, lam_k2, subln_w, sgu_ln_g, sgu_ln_b, w_s, b_s,
           conv_w, w_out, final_norm_w):
    b, seq, d = x.shape
    depth = norm_w.shape[0]
    tables = _rotary_tables(seq)
    x2 = x.reshape(b * seq, d)
    fnw = final_norm_w.reshape(1, d)
    norm_w3 = norm_w.reshape(depth, 1, d)
    lamv = jnp.stack([lam_q1, lam_k1, lam_q2, lam_k2], axis=1)
    subln_w3 = subln_w.reshape(depth, 1, HEAD_W)
    ln_g3 = sgu_ln_g.reshape(depth, 1, B_WIDTH)
    ln_b3 = sgu_ln_b.reshape(depth, 1, B_WIDTH)
    bs_b = jnp.broadcast_to(b_s[..., None], (depth, B_GROUPS, B_CHUNK, LANES))
    w_out_bf16 = w_out.astype(BF16)
    for l in range(depth):
        lam_init = 0.8 - 0.6 * math.exp(-0.3 * l)
        proj2 = _project(x2, norm_w3, w_in, l)
        ya = _attention(proj2.reshape(b, seq, PROJ_COLS), lamv, subln_w3, tables, l, lam_init)
        x2 = _mix_out(ya.reshape(b * seq, A_WIDTH), proj2, x2, ln_g3, ln_b3, w_s, bs_b, conv_w,
                      w_out_bf16, l, fnw, seq, l == depth - 1)
    return x2.reshape(b, seq, d)
```

```python
import functools
import math

import jax
import jax.numpy as jnp
from jax import lax
from jax.experimental import pallas as pl
from jax.experimental.pallas import tpu as pltpu

F32 = jnp.float32
BF16 = jnp.bfloat16

D_MODEL = 2048
A_HEADS = 8
QK_DIM = 64
HEAD_W = 2 * QK_DIM
A_WIDTH = A_HEADS * HEAD_W
B_GROUPS = 4
B_CHUNK = 128
B_WIDTH = 512
C_WIDTH = 512
CONV_K = 3
PROJ_COLS = 4 * A_WIDTH + 3 * B_WIDTH + 4 * C_WIDTH
ROPE_THETA = 10000.0
NORM_EPS = 1e-5
LN_EPS = 1e-5
LOG2E = 1.4426950408889634

LANES = 128
VMEM_LIMIT = 56 * 1024 * 1024

PROJ_TM = 1024
PROJ_TN = 512
NORM_ROWS = 128
ATT_TK = 512
ATT_NSUB = 2
ATT_TQ = ATT_NSUB * ATT_TK
ATT_CW = 512
BF16_SUBLANES = 16
VT_ROWS = HEAD_W + BF16_SUBLANES
MIX_TM = 512
HALO = 8

NEG_BIG = -1e30


def _silu(g):
    return g * jax.nn.sigmoid(g)


def _proj_kernel(x_ref, nw_ref, w_ref, o_ref, h_scr):
    j = pl.program_id(1)

    @pl.when(j == 0)
    def _():
        nw = nw_ref[...]

        def norm_rows(r, carry):
            rows = pl.ds(pl.multiple_of(r * NORM_ROWS, NORM_ROWS), NORM_ROWS)
            x = x_ref[rows, :]
            ms = jnp.mean(x * x, axis=-1, keepdims=True)
            h_scr[rows, :] = (x * lax.rsqrt(ms + NORM_EPS) * nw).astype(BF16)
            return carry

        lax.fori_loop(0, PROJ_TM // NORM_ROWS, norm_rows, 0)

    o_ref[...] = jnp.dot(h_scr[...], w_ref[...].astype(BF16), preferred_element_type=F32).astype(BF16)


def _project(x2, norm_w, w_in, layer):
    n_rows = x2.shape[0]
    return pl.pallas_call(
        _proj_kernel,
        out_shape=jax.ShapeDtypeStruct((n_rows, PROJ_COLS), BF16),
        grid=(n_rows // PROJ_TM, PROJ_COLS // PROJ_TN),
        in_specs=[
            pl.BlockSpec((PROJ_TM, D_MODEL), lambda i, j: (i, 0)),
            pl.BlockSpec((None, 1, D_MODEL), lambda i, j: (layer, 0, 0)),
            pl.BlockSpec((None, D_MODEL, PROJ_TN), lambda i, j: (layer, 0, j)),
        ],
        out_specs=pl.BlockSpec((PROJ_TM, PROJ_TN), lambda i, j: (i, j)),
        scratch_shapes=[pltpu.VMEM((PROJ_TM, D_MODEL), BF16)],
        compiler_params=pltpu.CompilerParams(
            dimension_semantics=("arbitrary", "arbitrary"), vmem_limit_bytes=VMEM_LIMIT),
        name="proj",
    )(x2, norm_w, w_in)


def _rotary(x, cos, sin):
    half = QK_DIM // 2
    lane = lax.broadcasted_iota(jnp.int32, x.shape, 1)
    partner = jnp.where(lane % QK_DIM < half, pltpu.roll(x, LANES - half, 1), pltpu.roll(x, half, 1))
    return x * cos + partner * sin


def _tile_rows(tile, size):
    if isinstance(tile, int):
        return pl.ds(tile * size, size)
    return pl.ds(pl.multiple_of(tile * size, size), size)


def _attn_kernel(lamv_ref, sw_ref, tab_ref, q_ref, qn_ref, k_ref, v_ref, g_ref, o_ref,
                 kr_scr, vt_scr, qm_scr, s_scr, acc_scr, bm_scr, mask_scr, *, lam_init, seq):
    tk, tq, nsub = ATT_TK, ATT_TQ, ATT_NSUB
    qi = pl.program_id(2)
    n_q = seq // tq
    slot = qi & 1
    nt = (((1,), (1,)), ((), ()))

    def prep_q(src_ref, tile, dst):
        rows = _tile_rows(tile, tq)
        qf = _rotary(src_ref[0].astype(F32), tab_ref[0, rows, :], tab_ref[1, rows, :]) * (QK_DIM ** -0.5 * LOG2E)
        lane = lax.broadcasted_iota(jnp.int32, (tq, HEAD_W), 1)
        is_c1 = lane < QK_DIM
        qm_scr[dst, 0] = jnp.where(is_c1, qf, 0.0).astype(BF16)
        qm_scr[dst, 1] = jnp.where(is_c1, 0.0, qf).astype(BF16)

    def produce(kt, buf, q_slot, col0):
        kb = kr_scr[_tile_rows(kt, tk), :]
        bmax = []
        for c in range(2):
            s = lax.dot_general(kb, qm_scr[q_slot, c, col0:, :], nt, preferred_element_type=F32)
            s_scr[buf, c, :, col0:] = s
            bmax.append(jnp.max(s, axis=0, keepdims=True))
        return tuple(bmax)

    def consume(kt, buf, stats, bmax, col0, diagonal):
        vtb = vt_scr[kt]
        new = []
        for c in range(2):
            m_all = stats[c]
            m_old = m_all[:, col0:]
            s = s_scr[buf, c, :, col0:]
            bm = bmax[c]
            if diagonal:
                sm = s[:, :tk] + mask_scr[...]
                bm_m = jnp.max(sm, axis=0, keepdims=True)
                if s.shape[1] > tk:
                    s = jnp.concatenate([sm, s[:, tk:]], axis=1)
                    bm = jnp.concatenate([bm_m, bm[:, tk:]], axis=1)
                else:
                    s, bm = sm, bm_m
            m_new = jnp.maximum(m_old, bm)
            alpha = jnp.exp2(m_old - m_new)
            for w0 in range(0, s.shape[1], ATT_CW):
                cols = slice(w0, w0 + ATT_CW)
                p = jnp.exp2(s[:, cols] - m_new[:, cols]).astype(BF16)
                acc_cols = slice(col0 + w0, col0 + w0 + ATT_CW)
                acc_scr[c, :, acc_cols] = (alpha[:, cols] * acc_scr[c, :, acc_cols]
                                           + jnp.dot(vtb, p, preferred_element_type=F32))
            if col0 > 0:
                m_new = jnp.concatenate([m_all[:, :col0], m_new], axis=1)
            new.append(m_new)
        return tuple(new)

    @pl.when(qi == 0)
    def _():
        row = lax.broadcasted_iota(jnp.int32, (tk, tk), 0)
        col = lax.broadcasted_iota(jnp.int32, (tk, tk), 1)
        mask_scr[...] = jnp.where(row <= col, 0.0, NEG_BIG)
        for c in range(seq // tk):
            rows = slice(c * tk, (c + 1) * tk)
            vt_scr[c, :HEAD_W, :] = v_ref[0, rows, :].T
            vt_scr[c, HEAD_W:, :] = jnp.ones((VT_ROWS - HEAD_W, tk), BF16)
            kr_scr[rows, :] = _rotary(k_ref[0, rows, :].astype(F32), tab_ref[0, rows, :],
                                      tab_ref[1, rows, :]).astype(BF16)
        prep_q(q_ref, 0, 0)
        bm0 = produce(0, 0, 0, 0)
        bm_scr[0] = bm0[0]
        bm_scr[1] = bm0[1]

    acc_scr[...] = jnp.zeros_like(acc_scr)

    def pair(j, carry):
        stats, bm_a = carry[:2], carry[2:]
        bm_b = produce(2 * j + 1, 1, slot, 0)
        stats = consume(2 * j, 0, stats, bm_a, 0, False)
        bm_a = produce(2 * j + 2, 0, slot, 0)
        stats = consume(2 * j + 1, 1, stats, bm_b, 0, False)
        return stats + bm_a

    n_full = nsub * qi
    stats0 = (jnp.full((1, tq), NEG_BIG, F32),) * 2
    carry = lax.fori_loop(0, qi * (nsub // 2), pair, stats0 + (bm_scr[0], bm_scr[1]))

    def tail(with_next):
        stats, bm = carry[:2], carry[2:]
        for r in range(nsub):
            bm_next = produce(n_full + r + 1, (r + 1) % 2, slot, (r + 1) * tk) if r + 1 < nsub else None
            stats = consume(n_full + r, r % 2, stats, bm, r * tk, True)
            bm = bm_next

        lamv = lamv_ref[...]
        lam = (jnp.exp(jnp.sum(lamv[0:1] * lamv[1:2], axis=-1, keepdims=True))
               - jnp.exp(jnp.sum(lamv[2:3] * lamv[3:4], axis=-1, keepdims=True)) + lam_init)
        l1 = acc_scr[0, HEAD_W:HEAD_W + 1, :]
        l2 = acc_scr[1, HEAD_W:HEAD_W + 1, :]
        d = acc_scr[0, :HEAD_W, :] * (1.0 / l1) - lam * (acc_scr[1, :HEAD_W, :] * (1.0 / l2))
        ms = jnp.mean(d * d, axis=0, keepdims=True)
        y = (d * lax.rsqrt(ms + NORM_EPS)).T
        g = g_ref[0].astype(F32)
        o_ref[0] = (y * (sw_ref[...] * (1.0 - lam_init)) * _silu(g)).astype(BF16)

        if with_next:
            prep_q(qn_ref, qi + 1, 1 - slot)
            bm0 = produce(0, 0, 1 - slot, 0)
            bm_scr[0] = bm0[0]
            bm_scr[1] = bm0[1]

    pl.when(qi < n_q - 1)(lambda: tail(True))
    pl.when(qi == n_q - 1)(lambda: tail(False))


def _attention(proj3, lamv, subln_w, tables, layer, lam_init):
    b, seq, _ = proj3.shape
    tk, tq = ATT_TK, ATT_TQ
    n_q = seq // tq
    head_blk = lambda off: pl.BlockSpec((1, seq, HEAD_W), lambda bi, h, qi: (bi, 0, off + h))
    tile_blk = lambda off: pl.BlockSpec((1, tq, HEAD_W), lambda bi, h, qi: (bi, qi, off + h))
    return pl.pallas_call(
        functools.partial(_attn_kernel, lam_init=lam_init, seq=seq),
        out_shape=jax.ShapeDtypeStruct((b, seq, A_WIDTH), BF16),
        grid=(b, A_HEADS, n_q),
        in_specs=[
            pl.BlockSpec((None, 4, QK_DIM), lambda bi, h, qi: (layer, 0, 0)),
            pl.BlockSpec((None, 1, HEAD_W), lambda bi, h, qi: (layer, 0, 0)),
            pl.BlockSpec((2, seq, LANES), lambda bi, h, qi: (0, 0, 0)),
            tile_blk(0),
            pl.BlockSpec((1, tq, HEAD_W), lambda bi, h, qi: (bi, jnp.minimum(qi + 1, n_q - 1), h)),
            head_blk(A_HEADS),
            head_blk(2 * A_HEADS),
            tile_blk(3 * A_HEADS),
        ],
        out_specs=pl.BlockSpec((1, tq, HEAD_W), lambda bi, h, qi: (bi, qi, h)),
        scratch_shapes=[
            pltpu.VMEM((seq, HEAD_W), BF16),
            pltpu.VMEM((seq // tk, VT_ROWS, tk), BF16),
            pltpu.VMEM((2, 2, tq, HEAD_W), BF16),
            pltpu.VMEM((2, 2, tk, tq), F32),
            pltpu.VMEM((2, VT_ROWS, tq), F32),
            pltpu.VMEM((2, 1, tq), F32),
            pltpu.VMEM((tk, tk), F32),
        ],
        compiler_params=pltpu.CompilerParams(
            dimension_semantics=("arbitrary", "arbitrary", "arbitrary"), vmem_limit_bytes=VMEM_LIMIT),
        name="diff_attn",
    )(lamv, subln_w, tables, proj3, proj3, proj3, proj3, proj3)


def _mix_kernel(ya_ref, u_ref, vs_ref, gb_ref, xc_ref, bg_ref, cg_ref, gc_ref, xcp_ref, cgp_ref,
                lng_ref, lnb_ref, ws_ref, bs_ref, cw_ref, wout_ref, x_ref, fnw_ref, o_ref,
                cat_scr, z_scr, *, is_last, tiles_per_seq):
    tm = MIX_TM
    n_chunks = tm // B_CHUNK
    i = pl.program_id(0)

    seq_start = (i % tiles_per_seq) == 0
    zp = cgp_ref[...].astype(F32) * xcp_ref[...].astype(F32)
    z_scr[0:HALO, :] = jnp.where(seq_start, 0.0, zp)
    z_scr[HALO:HALO + tm, :] = cg_ref[...].astype(F32) * xc_ref[...].astype(F32)

    tri_row = lax.broadcasted_iota(jnp.int32, (B_CHUNK, B_CHUNK), 0)
    tri_col = lax.broadcasted_iota(jnp.int32, (B_CHUNK, B_CHUNK), 1)
    w_tril = [jnp.where(tri_row >= tri_col, ws_ref[g], 0.0).astype(BF16) for g in range(B_GROUPS)]
    cw = cw_ref[...]

    def mix_b(c):
        rows = slice(c * B_CHUNK, (c + 1) * B_CHUNK)
        v = vs_ref[rows, :].astype(F32)
        mu = jnp.mean(v, axis=-1, keepdims=True)
        vc = v - mu
        var = jnp.mean(vc * vc, axis=-1, keepdims=True)
        vn = (vc * lax.rsqrt(var + LN_EPS) * lng_ref[...] + lnb_ref[...]).astype(BF16)
        for g in range(B_GROUPS):
            cols = slice(g * LANES, (g + 1) * LANES)
            mixed = jnp.dot(w_tril[g], vn[:, cols], preferred_element_type=F32) + bs_ref[g]
            yb = u_ref[rows, cols].astype(F32) * mixed * _silu(gb_ref[rows, cols].astype(F32))
            cat_scr[rows, cols] = yb.astype(BF16)

    def mix_c(c):
        rows = slice(c * B_CHUNK, (c + 1) * B_CHUNK)
        conv = cw[CONV_K - 1:CONV_K] * z_scr[HALO + c * B_CHUNK:HALO + (c + 1) * B_CHUNK, :]
        for tap in range(CONV_K - 1):
            start = HALO + c * B_CHUNK - (CONV_K - 1 - tap)
            conv = conv + cw[tap:tap + 1] * z_scr[start:start + B_CHUNK, :]
        yc = bg_ref[rows, :].astype(F32) * conv * _silu(gc_ref[rows, :].astype(F32))
        cat_scr[rows, B_WIDTH:B_WIDTH + C_WIDTH] = yc.astype(BF16)

    cw_out = D_MODEL // n_chunks
    w_b = slice(A_WIDTH, A_WIDTH + B_WIDTH)
    w_c = slice(A_WIDTH + B_WIDTH, D_MODEL)
    for c in range(n_chunks):
        cols = slice(c * cw_out, (c + 1) * cw_out)
        o_ref[:, cols] = x_ref[:, cols] + jnp.dot(ya_ref[...], wout_ref[0:A_WIDTH, cols],
                                                  preferred_element_type=F32)
        mix_b(c)
    for c in range(n_chunks):
        cols = slice(c * cw_out, (c + 1) * cw_out)
        o_ref[:, cols] += jnp.dot(cat_scr[:, 0:B_WIDTH], wout_ref[w_b, cols], preferred_element_type=F32)
        mix_c(c)
    out = o_ref[...] + jnp.dot(cat_scr[:, B_WIDTH:], wout_ref[w_c, :], preferred_element_type=F32)
    if is_last:
        ms = jnp.mean(out * out, axis=-1, keepdims=True)
        out = out * lax.rsqrt(ms + NORM_EPS) * fnw_ref[...]
    o_ref[...] = out


def _mix_out(ya2, proj2, x2, ln_g, ln_b, w_s, bs_b, conv_w, wout_bf16, layer, fnw, seq, is_last):
    n_rows = x2.shape[0]
    tm = MIX_TM
    base = (4 * A_WIDTH) // B_WIDTH
    pblk = lambda k: pl.BlockSpec((tm, B_WIDTH), lambda i: (i, base + k))
    halo = lambda k: pl.BlockSpec((HALO, B_WIDTH), lambda i: (jnp.maximum(i * (tm // HALO) - 1, 0), base + k))
    per_layer = lambda shape: pl.BlockSpec((None,) + shape, lambda i: (layer,) + (0,) * len(shape))
    return pl.pallas_call(
        functools.partial(_mix_kernel, is_last=is_last, tiles_per_seq=seq // tm),
        out_shape=jax.ShapeDtypeStruct((n_rows, D_MODEL), F32),
        grid=(n_rows // tm,),
        in_specs=[
            pl.BlockSpec((tm, A_WIDTH), lambda i: (i, 0)),
            pblk(0), pblk(1), pblk(2),
            pblk(3), pblk(4), pblk(5), pblk(6),
            halo(3), halo(5),
            per_layer((1, B_WIDTH)), per_layer((1, B_WIDTH)),
            per_layer((B_GROUPS, B_CHUNK, B_CHUNK)),
            per_layer((B_GROUPS, B_CHUNK, LANES)),
            per_layer((CONV_K, C_WIDTH)),
            per_layer((D_MODEL, D_MODEL)),
            pl.BlockSpec((tm, D_MODEL), lambda i: (i, 0)),
            pl.BlockSpec((1, D_MODEL), lambda i: (0, 0)),
        ],
        out_specs=pl.BlockSpec((tm, D_MODEL), lambda i: (i, 0)),
        scratch_shapes=[
            pltpu.VMEM((tm, B_WIDTH + C_WIDTH), BF16),
            pltpu.VMEM((HALO + tm, C_WIDTH), F32),
        ],
        compiler_params=pltpu.CompilerParams(
            dimension_semantics=("arbitrary",), vmem_limit_bytes=VMEM_LIMIT),
        name="mix_out",
    )(ya2, proj2, proj2, proj2, proj2, proj2, proj2, proj2, proj2, proj2,
      ln_g, ln_b, w_s, bs_b, conv_w, wout_bf16, x2, fnw)


def _rotary_tables(seq):
    half = QK_DIM // 2
    pos = jnp.arange(seq, dtype=F32)
    inv_freq = ROPE_THETA ** (-jnp.arange(0, QK_DIM, 2, dtype=F32) / QK_DIM)
    ang = pos[:, None] * inv_freq[None, :]
    cos = jnp.tile(jnp.cos(ang), (1, LANES // half))
    sin = jnp.tile(jnp.sin(ang), (1, LANES // half))
    sign = jnp.where(jnp.arange(LANES) % QK_DIM < half, -1.0, 1.0).astype(F32)
    return jnp.stack([cos, sin * sign])


def kernel(x, norm_w, w_in, lam_q1, lam_k1, lam_q2, lam_k2, subln_w, sgu_ln_g, sgu_ln_b, w_s, b_s,
           conv_w, w_out, final_norm_w):
    b, seq, d = x.shape
    depth = norm_w.shape[0]
    tables = _rotary_tables(seq)
    x2 = x.reshape(b * seq, d)
    fnw = final_norm_w.reshape(1, d)
    norm_w3 = norm_w.reshape(depth, 1, d)
    lamv = jnp.stack([lam_q1, lam_k1, lam_q2, lam_k2], axis=1)
    subln_w3 = subln_w.reshape(depth, 1, HEAD_W)
    ln_g3 = sgu_ln_g.reshape(depth, 1, B_WIDTH)
    ln_b3 = sgu_ln_b.reshape(depth, 1, B_WIDTH)
    bs_b = jnp.broadcast_to(b_s[..., None], (depth, B_GROUPS, B_CHUNK, LANES))
    w_out_bf16 = w_out.astype(BF16)
    for l in range(depth):
        lam_init = 0.8 - 0.6 * math.exp(-0.3 * l)
        proj2 = _project(x2, norm_w3, w_in, l)
        ya = _attention(proj2.reshape(b, seq, PROJ_COLS), lamv, subln_w3, tables, l, lam_init)
        x2 = _mix_out(ya.reshape(b * seq, A_WIDTH), proj2, x2, ln_g3, ln_b3, w_s, bs_b, conv_w,
                      w_out_bf16, l, fnw, seq, l == depth - 1)
    return x2.reshape(b, seq, d)
```

```python
import functools
import math

import jax
import jax.numpy as jnp
from jax import lax
from jax.experimental import pallas as pl
from jax.experimental.pallas import tpu as pltpu

F32 = jnp.float32
BF16 = jnp.bfloat16

D_MODEL = 2048
A_HEADS = 8
QK_DIM = 64
HEAD_W = 2 * QK_DIM
A_WIDTH = A_HEADS * HEAD_W
B_GROUPS = 4
B_CHUNK = 128
B_WIDTH = 512
C_WIDTH = 512
CONV_K = 3
PROJ_COLS = 4 * A_WIDTH + 3 * B_WIDTH + 4 * C_WIDTH
ROPE_THETA = 10000.0
NORM_EPS = 1e-5
LN_EPS = 1e-5
LOG2E = 1.4426950408889634

LANES = 128
VMEM_LIMIT = 56 * 1024 * 1024

PROJ_TM = 2048
PROJ_XB = 1024
PROJ_NORM_STEPS = PROJ_TM // PROJ_XB
PROJ_TN = 512
NORM_ROWS = 128
ATT_TK = 512
ATT_NSUB = 2
ATT_TQ = ATT_NSUB * ATT_TK
ATT_CW = 512
BF16_SUBLANES = 16
VT_ROWS = HEAD_W + BF16_SUBLANES
MIX_TM = 512
HALO = 8

NEG_BIG = -1e30


def _silu(g):
    return g * jax.nn.sigmoid(g)


def _proj_kernel(x_ref, nw_ref, w_ref, o_ref, h_scr):
    j = pl.program_id(1)

    @pl.when(j < PROJ_NORM_STEPS)
    def _():
        nw = nw_ref[...]

        def norm_rows(r, carry):
            x = x_ref[pl.ds(pl.multiple_of(r * NORM_ROWS, NORM_ROWS), NORM_ROWS), :]
            ms = jnp.mean(x * x, axis=-1, keepdims=True)
            dst = pl.ds(pl.multiple_of(j * PROJ_XB + r * NORM_ROWS, NORM_ROWS), NORM_ROWS)
            h_scr[dst, :] = (x * lax.rsqrt(ms + NORM_EPS) * nw).astype(BF16)
            return carry

        lax.fori_loop(0, PROJ_XB // NORM_ROWS, norm_rows, 0)

    @pl.when(j >= PROJ_NORM_STEPS)
    def _():
        o_ref[...] = jnp.dot(h_scr[...], w_ref[...].astype(BF16), preferred_element_type=F32).astype(BF16)


def _project(x2, norm_w, w_in, layer):
    n_rows = x2.shape[0]
    col_tile = lambda j: jnp.maximum(j - PROJ_NORM_STEPS, 0)
    return pl.pallas_call(
        _proj_kernel,
        out_shape=jax.ShapeDtypeStruct((n_rows, PROJ_COLS), BF16),
        grid=(n_rows // PROJ_TM, PROJ_NORM_STEPS + PROJ_COLS // PROJ_TN),
        in_specs=[
            pl.BlockSpec((PROJ_XB, D_MODEL),
                         lambda i, j: (i * PROJ_NORM_STEPS + jnp.minimum(j, PROJ_NORM_STEPS - 1), 0)),
            pl.BlockSpec((None, 1, D_MODEL), lambda i, j: (layer, 0, 0)),
            pl.BlockSpec((None, D_MODEL, PROJ_TN), lambda i, j: (layer, 0, col_tile(j))),
        ],
        out_specs=pl.BlockSpec((PROJ_TM, PROJ_TN), lambda i, j: (i, col_tile(j))),
        scratch_shapes=[pltpu.VMEM((PROJ_TM, D_MODEL), BF16)],
        compiler_params=pltpu.CompilerParams(
            dimension_semantics=("arbitrary", "arbitrary"), vmem_limit_bytes=VMEM_LIMIT),
        name="proj",
    )(x2, norm_w, w_in)


def _rotary(x, cos, sin):
    half = QK_DIM // 2
    lane = lax.broadcasted_iota(jnp.int32, x.shape, 1)
    partner = jnp.where(lane % QK_DIM < half, pltpu.roll(x, LANES - half, 1), pltpu.roll(x, half, 1))
    return x * cos + partner * sin


def _tile_rows(tile, size):
    if isinstance(tile, int):
        return pl.ds(tile * size, size)
    return pl.ds(pl.multiple_of(tile * size, size), size)


def _attn_kernel(lamv_ref, sw_ref, tab_ref, q_ref, qn_ref, k_ref, v_ref, g_ref, o_ref,
                 kr_scr, vt_scr, qm_scr, s_scr, acc_scr, bm_scr, mask_scr, *, lam_init, seq):
    tk, tq, nsub = ATT_TK, ATT_TQ, ATT_NSUB
    qi = pl.program_id(2)
    n_q = seq // tq
    slot = qi & 1
    nt = (((1,), (1,)), ((), ()))

    def prep_q(src_ref, tile, dst):
        rows = _tile_rows(tile, tq)
        qf = _rotary(src_ref[0].astype(F32), tab_ref[0, rows, :], tab_ref[1, rows, :]) * (QK_DIM ** -0.5 * LOG2E)
        lane = lax.broadcasted_iota(jnp.int32, (tq, HEAD_W), 1)
        is_c1 = lane < QK_DIM
        qm_scr[dst, 0] = jnp.where(is_c1, qf, 0.0).astype(BF16)
        qm_scr[dst, 1] = jnp.where(is_c1, 0.0, qf).astype(BF16)

    def produce(kt, buf, q_slot, col0, diagonal=False):
        kb = kr_scr[_tile_rows(kt, tk), :]
        bmax = []
        for c in range(2):
            s = lax.dot_general(kb, qm_scr[q_slot, c, col0:, :], nt, preferred_element_type=F32)
            if diagonal:
                sm = s[:, :tk] + mask_scr[...]
                s = jnp.concatenate([sm, s[:, tk:]], axis=1) if s.shape[1] > tk else sm
            s_scr[buf, c, :, col0:] = s
            bmax.append(jnp.max(s, axis=0, keepdims=True))
        return tuple(bmax)

    def consume(kt, buf, stats, bmax, col0, diagonal):
        vtb = vt_scr[kt]
        new = []
        for c in range(2):
            m_all = stats[c]
            m_old = m_all[:, col0:]
            s = s_scr[buf, c, :, col0:]
            bm = bmax[c]
            if diagonal:
                sm = s[:, :tk] + mask_scr[...]
                bm_m = jnp.max(sm, axis=0, keepdims=True)
                if s.shape[1] > tk:
                    s = jnp.concatenate([sm, s[:, tk:]], axis=1)
                    bm = jnp.concatenate([bm_m, bm[:, tk:]], axis=1)
                else:
                    s, bm = sm, bm_m
            m_new = jnp.maximum(m_old, bm)
            alpha = jnp.exp2(m_old - m_new)
            for w0 in range(0, s.shape[1], ATT_CW):
                cols = slice(w0, w0 + ATT_CW)
                p = jnp.exp2(s[:, cols] - m_new[:, cols]).astype(BF16)
                acc_cols = slice(col0 + w0, col0 + w0 + ATT_CW)
                acc_scr[c, :, acc_cols] = (alpha[:, cols] * acc_scr[c, :, acc_cols]
                                           + jnp.dot(vtb, p, preferred_element_type=F32))
            if col0 > 0:
                m_new = jnp.concatenate([m_all[:, :col0], m_new], axis=1)
            new.append(m_new)
        return tuple(new)

    @pl.when((pl.program_id(0) == 0) & (pl.program_id(1) == 0) & (qi == 0))
    def _():
        row = lax.broadcasted_iota(jnp.int32, (tk, tk), 0)
        col = lax.broadcasted_iota(jnp.int32, (tk, tk), 1)
        mask_scr[...] = jnp.where(row <= col, 0.0, NEG_BIG)

    @pl.when(qi == 0)
    def _():
        for c in range(seq // tk):
            rows = slice(c * tk, (c + 1) * tk)
            vt_scr[c, :HEAD_W, :] = v_ref[0, rows, :].T
            vt_scr[c, HEAD_W:, :] = jnp.ones((VT_ROWS - HEAD_W, tk), BF16)
            kr_scr[rows, :] = _rotary(k_ref[0, rows, :].astype(F32), tab_ref[0, rows, :],
                                      tab_ref[1, rows, :]).astype(BF16)
        prep_q(q_ref, 0, 0)
        bm0 = produce(0, 0, 0, 0)
        bm_scr[0] = bm0[0]
        bm_scr[1] = bm0[1]

    acc_scr[...] = jnp.zeros_like(acc_scr)

    def pair(j, carry):
        stats, bm_a = carry[:2], carry[2:]
        bm_b = produce(2 * j + 1, 1, slot, 0)
        stats = consume(2 * j, 0, stats, bm_a, 0, False)
        bm_a = produce(2 * j + 2, 0, slot, 0)
        stats = consume(2 * j + 1, 1, stats, bm_b, 0, False)
        return stats + bm_a

    n_full = nsub * qi
    stats0 = (jnp.full((1, tq), NEG_BIG, F32),) * 2
    carry = lax.fori_loop(0, qi * (nsub // 2), pair, stats0 + (bm_scr[0], bm_scr[1]))

    def tail(with_next):
        stats, bm = carry[:2], carry[2:]
        for r in range(nsub):
            bm_next = (produce(n_full + r + 1, (r + 1) % 2, slot, (r + 1) * tk, diagonal=True)
                       if r + 1 < nsub else None)
            stats = consume(n_full + r, r % 2, stats, bm, r * tk, diagonal=(r == 0))
            bm = bm_next

        lamv = lamv_ref[...]
        lam = (jnp.exp(jnp.sum(lamv[0:1] * lamv[1:2], axis=-1, keepdims=True))
               - jnp.exp(jnp.sum(lamv[2:3] * lamv[3:4], axis=-1, keepdims=True)) + lam_init)
        l1 = acc_scr[0, HEAD_W:HEAD_W + 1, :]
        l2 = acc_scr[1, HEAD_W:HEAD_W + 1, :]
        d = acc_scr[0, :HEAD_W, :] * (1.0 / l1) - lam * (acc_scr[1, :HEAD_W, :] * (1.0 / l2))
        ms = jnp.mean(d * d, axis=0, keepdims=True)
        y = (d * lax.rsqrt(ms + NORM_EPS)).T
        g = g_ref[0].astype(F32)
        o_ref[0] = (y * (sw_ref[...] * (1.0 - lam_init)) * _silu(g)).astype(BF16)

        if with_next:
            prep_q(qn_ref, qi + 1, 1 - slot)
            bm0 = produce(0, 0, 1 - slot, 0)
            bm_scr[0] = bm0[0]
            bm_scr[1] = bm0[1]

    pl.when(qi < n_q - 1)(lambda: tail(True))
    pl.when(qi == n_q - 1)(lambda: tail(False))


def _attention(proj3, lamv, subln_w, tables, layer, lam_init):
    b, seq, _ = proj3.shape
    tk, tq = ATT_TK, ATT_TQ
    n_q = seq // tq
    head_blk = lambda off: pl.BlockSpec((1, seq, HEAD_W), lambda bi, h, qi: (bi, 0, off + h))
    tile_blk = lambda off: pl.BlockSpec((1, tq, HEAD_W), lambda bi, h, qi: (bi, qi, off + h))
    return pl.pallas_call(
        functools.partial(_attn_kernel, lam_init=lam_init, seq=seq),
        out_shape=jax.ShapeDtypeStruct((b, seq, A_WIDTH), BF16),
        grid=(b, A_HEADS, n_q),
        in_specs=[
            pl.BlockSpec((None, 4, QK_DIM), lambda bi, h, qi: (layer, 0, 0)),
            pl.BlockSpec((None, 1, HEAD_W), lambda bi, h, qi: (layer, 0, 0)),
            pl.BlockSpec((2, seq, LANES), lambda bi, h, qi: (0, 0, 0)),
            tile_blk(0),
            pl.BlockSpec((1, tq, HEAD_W), lambda bi, h, qi: (bi, jnp.minimum(qi + 1, n_q - 1), h)),
            head_blk(A_HEADS),
            head_blk(2 * A_HEADS),
            tile_blk(3 * A_HEADS),
        ],
        out_specs=pl.BlockSpec((1, tq, HEAD_W), lambda bi, h, qi: (bi, qi, h)),
        scratch_shapes=[
            pltpu.VMEM((seq, HEAD_W), BF16),
            pltpu.VMEM((seq // tk, VT_ROWS, tk), BF16),
            pltpu.VMEM((2, 2, tq, HEAD_W), BF16),
            pltpu.VMEM((2, 2, tk, tq), F32),
            pltpu.VMEM((2, VT_ROWS, tq), F32),
            pltpu.VMEM((2, 1, tq), F32),
            pltpu.VMEM((tk, tk), F32),
        ],
        compiler_params=pltpu.CompilerParams(
            dimension_semantics=("arbitrary", "arbitrary", "arbitrary"), vmem_limit_bytes=VMEM_LIMIT),
        name="diff_attn",
    )(lamv, subln_w, tables, proj3, proj3, proj3, proj3, proj3)


def _mix_kernel(ya_ref, u_ref, vs_ref, gb_ref, xc_ref, bg_ref, cg_ref, gc_ref, xcp_ref, cgp_ref,
                lng_ref, lnb_ref, ws_ref, bs_ref, cw_ref, wout_ref, x_ref, fnw_ref, o_ref,
                cat_scr, z_scr, *, is_last, tiles_per_seq):
    tm = MIX_TM
    n_chunks = tm // B_CHUNK
    i = pl.program_id(0)

    seq_start = (i % tiles_per_seq) == 0
    zp = cgp_ref[...].astype(F32) * xcp_ref[...].astype(F32)
    z_scr[0:HALO, :] = jnp.where(seq_start, 0.0, zp)
    z_scr[HALO:HALO + tm, :] = cg_ref[...].astype(F32) * xc_ref[...].astype(F32)

    tri_row = lax.broadcasted_iota(jnp.int32, (B_CHUNK, B_CHUNK), 0)
    tri_col = lax.broadcasted_iota(jnp.int32, (B_CHUNK, B_CHUNK), 1)
    w_tril = [jnp.where(tri_row >= tri_col, ws_ref[g], 0.0).astype(BF16) for g in range(B_GROUPS)]
    cw = cw_ref[...]

    def mix_b(c):
        rows = slice(c * B_CHUNK, (c + 1) * B_CHUNK)
        v = vs_ref[rows, :].astype(F32)
        mu = jnp.mean(v, axis=-1, keepdims=True)
        vc = v - mu
        var = jnp.mean(vc * vc, axis=-1, keepdims=True)
        vn = (vc * lax.rsqrt(var + LN_EPS) * lng_ref[...] + lnb_ref[...]).astype(BF16)
        for g in range(B_GROUPS):
            cols = slice(g * LANES, (g + 1) * LANES)
            mixed = jnp.dot(w_tril[g], vn[:, cols], preferred_element_type=F32) + bs_ref[g]
            yb = u_ref[rows, cols].astype(F32) * mixed * _silu(gb_ref[rows, cols].astype(F32))
            cat_scr[rows, cols] = yb.astype(BF16)

    def mix_c(c):
        rows = slice(c * B_CHUNK, (c + 1) * B_CHUNK)
        conv = cw[CONV_K - 1:CONV_K] * z_scr[HALO + c * B_CHUNK:HALO + (c + 1) * B_CHUNK, :]
        for tap in range(CONV_K - 1):
            start = HALO + c * B_CHUNK - (CONV_K - 1 - tap)
            conv = conv + cw[tap:tap + 1] * z_scr[start:start + B_CHUNK, :]
        yc = bg_ref[rows, :].astype(F32) * conv * _silu(gc_ref[rows, :].astype(F32))
        cat_scr[rows, B_WIDTH:B_WIDTH + C_WIDTH] = yc.astype(BF16)

    cw_out = D_MODEL // n_chunks
    w_b = slice(A_WIDTH, A_WIDTH + B_WIDTH)
    w_c = slice(A_WIDTH + B_WIDTH, D_MODEL)
    for c in range(n_chunks):
        cols = slice(c * cw_out, (c + 1) * cw_out)
        o_ref[:, cols] = x_ref[:, cols] + jnp.dot(ya_ref[...], wout_ref[0:A_WIDTH, cols],
                                                  preferred_element_type=F32)
        mix_b(c)
    for c in range(n_chunks):
        cols = slice(c * cw_out, (c + 1) * cw_out)
        o_ref[:, cols] += jnp.dot(cat_scr[:, 0:B_WIDTH], wout_ref[w_b, cols], preferred_element_type=F32)
        mix_c(c)
    out = o_ref[...] + jnp.dot(cat_scr[:, B_WIDTH:], wout_ref[w_c, :], preferred_element_type=F32)
    if is_last:
        ms = jnp.mean(out * out, axis=-1, keepdims=True)
        out = out * lax.rsqrt(ms + NORM_EPS) * fnw_ref[...]
    o_ref[...] = out


def _mix_out(ya2, proj2, x2, ln_g, ln_b, w_s, bs_b, conv_w, wout_bf16, layer, fnw, seq, is_last):
    n_rows = x2.shape[0]
    tm = MIX_TM
    base = (4 * A_WIDTH) // B_WIDTH
    pblk = lambda k: pl.BlockSpec((tm, B_WIDTH), lambda i: (i, base + k))
    halo = lambda k: pl.BlockSpec((HALO, B_WIDTH), lambda i: (jnp.maximum(i * (tm // HALO) - 1, 0), base + k))
    per_layer = lambda shape: pl.BlockSpec((None,) + shape, lambda i: (layer,) + (0,) * len(shape))
    return pl.pallas_call(
        functools.partial(_mix_kernel, is_last=is_last, tiles_per_seq=seq // tm),
        out_shape=jax.ShapeDtypeStruct((n_rows, D_MODEL), F32),
        grid=(n_rows // tm,),
        in_specs=[
            pl.BlockSpec((tm, A_WIDTH), lambda i: (i, 0)),
            pblk(0), pblk(1), pblk(2),
            pblk(3), pblk(4), pblk(5), pblk(6),
            halo(3), halo(5),
            per_layer((1, B_WIDTH)), per_layer((1, B_WIDTH)),
            per_layer((B_GROUPS, B_CHUNK, B_CHUNK)),
            per_layer((B_GROUPS, B_CHUNK, LANES)),
            per_layer((CONV_K, C_WIDTH)),
            per_layer((D_MODEL, D_MODEL)),
            pl.BlockSpec((tm, D_MODEL), lambda i: (i, 0)),
            pl.BlockSpec((1, D_MODEL), lambda i: (0, 0)),
        ],
        out_specs=pl.BlockSpec((tm, D_MODEL), lambda i: (i, 0)),
        scratch_shapes=[
            pltpu.VMEM((tm, B_WIDTH + C_WIDTH), BF16),
            pltpu.VMEM((HALO + tm, C_WIDTH), F32),
        ],
        compiler_params=pltpu.CompilerParams(
            dimension_semantics=("arbitrary",), vmem_limit_bytes=VMEM_LIMIT),
        name="mix_out",
    )(ya2, proj2, proj2, proj2, proj2, proj2, proj2, proj2, proj2, proj2,
      ln_g, ln_b, w_s, bs_b, conv_w, wout_bf16, x2, fnw)


def _rotary_tables(seq):
    half = QK_DIM // 2
    pos = jnp.arange(seq, dtype=F32)
    inv_freq = ROPE_THETA ** (-jnp.arange(0, QK_DIM, 2, dtype=F32) / QK_DIM)
    ang = pos[:, None] * inv_freq[None, :]
    cos = jnp.tile(jnp.cos(ang), (1, LANES // half))
    sin = jnp.tile(jnp.sin(ang), (1, LANES // half))
    sign = jnp.where(jnp.arange(LANES) % QK_DIM < half, -1.0, 1.0).astype(F32)
    return jnp.stack([cos, sin * sign])


def kernel(x, norm_w, w_in, lam_q1, lam_k1, lam_q2, lam_k2, subln_w, sgu_ln_g, sgu_ln_b, w_s, b_s,
           conv_w, w_out, final_norm_w):
    b, seq, d = x.shape
    depth = norm_w.shape[0]
    tables = _rotary_tables(seq)
    x2 = x.reshape(b * seq, d)
    fnw = final_norm_w.reshape(1, d)
    norm_w3 = norm_w.reshape(depth, 1, d)
    lamv = jnp.stack([lam_q1, lam_k1, lam_q2, lam_k2], axis=1)
    subln_w3 = subln_w.reshape(depth, 1, HEAD_W)
    ln_g3 = sgu_ln_g.reshape(depth, 1, B_WIDTH)
    ln_b3 = sgu_ln_b.reshape(depth, 1, B_WIDTH)
    bs_b = jnp.broadcast_to(b_s[..., None], (depth, B_GROUPS, B_CHUNK, LANES))
    w_out_bf16 = w_out.astype(BF16)
    for l in range(depth):
        lam_init = 0.8 - 0.6 * math.exp(-0.3 * l)
        proj2 = _project(x2, norm_w3, w_in, l)
        ya = _attention(proj2.reshape(b, seq, PROJ_COLS), lamv, subln_w3, tables, l, lam_init)
        x2 = _mix_out(ya.reshape(b * seq, A_WIDTH), proj2, x2, ln_g3, ln_b3, w_s, bs_b, conv_w,
                      w_out_bf16, l, fnw, seq, l == depth - 1)
    return x2.reshape(b, seq, d)
```

```python
import functools
import math

import jax
import jax.numpy as jnp
from jax import lax
from jax.experimental import pallas as pl
from jax.experimental.pallas import tpu as pltpu

F32 = jnp.float32
BF16 = jnp.bfloat16

D_MODEL = 2048
A_HEADS = 8
QK_DIM = 64
HEAD_W = 2 * QK_DIM
A_WIDTH = A_HEADS * HEAD_W
B_GROUPS = 4
B_CHUNK = 128
B_WIDTH = 512
C_WIDTH = 512
CONV_K = 3
PROJ_COLS = 4 * A_WIDTH + 3 * B_WIDTH + 4 * C_WIDTH
ROPE_THETA = 10000.0
NORM_EPS = 1e-5
LN_EPS = 1e-5
LOG2E = 1.4426950408889634

LANES = 128
VMEM_LIMIT = 56 * 1024 * 1024

PROJ_TM = 2048
PROJ_XB = 1024
PROJ_NORM_STEPS = PROJ_TM // PROJ_XB
PROJ_TN = 512
NORM_ROWS = 128
ATT_TK = 512
ATT_TQ = 2 * ATT_TK
BF16_SUBLANES = 16
VT_ROWS = HEAD_W + BF16_SUBLANES
MIX_TM = 512
HALO = 8

NEG_BIG = -1e30


def _silu(g):
    return g * jax.nn.sigmoid(g)


def _proj_kernel(x_ref, nw_ref, w_ref, o_ref, h_scr):
    j = pl.program_id(1)

    @pl.when(j < PROJ_NORM_STEPS)
    def _():
        nw = nw_ref[...]

        def norm_rows(r, carry):
            x = x_ref[pl.ds(pl.multiple_of(r * NORM_ROWS, NORM_ROWS), NORM_ROWS), :]
            ms = jnp.mean(x * x, axis=-1, keepdims=True)
            dst = pl.ds(pl.multiple_of(j * PROJ_XB + r * NORM_ROWS, NORM_ROWS), NORM_ROWS)
            h_scr[dst, :] = (x * lax.rsqrt(ms + NORM_EPS) * nw).astype(BF16)
            return carry

        lax.fori_loop(0, PROJ_XB // NORM_ROWS, norm_rows, 0)

    @pl.when(j >= PROJ_NORM_STEPS)
    def _():
        o_ref[...] = jnp.dot(h_scr[...], w_ref[...].astype(BF16), preferred_element_type=F32).astype(BF16)


def _project(x2, norm_w, w_in, layer):
    n_rows = x2.shape[0]
    col_tile = lambda j: jnp.maximum(j - PROJ_NORM_STEPS, 0)
    return pl.pallas_call(
        _proj_kernel,
        out_shape=jax.ShapeDtypeStruct((n_rows, PROJ_COLS), BF16),
        grid=(n_rows // PROJ_TM, PROJ_NORM_STEPS + PROJ_COLS // PROJ_TN),
        in_specs=[
            pl.BlockSpec((PROJ_XB, D_MODEL),
                         lambda i, j: (i * PROJ_NORM_STEPS + jnp.minimum(j, PROJ_NORM_STEPS - 1), 0)),
            pl.BlockSpec((None, 1, D_MODEL), lambda i, j: (layer, 0, 0)),
            pl.BlockSpec((None, D_MODEL, PROJ_TN), lambda i, j: (layer, 0, col_tile(j))),
        ],
        out_specs=pl.BlockSpec((PROJ_TM, PROJ_TN), lambda i, j: (i, col_tile(j))),
        scratch_shapes=[pltpu.VMEM((PROJ_TM, D_MODEL), BF16)],
        compiler_params=pltpu.CompilerParams(
            dimension_semantics=("arbitrary", "arbitrary"), vmem_limit_bytes=VMEM_LIMIT),
        name="proj",
    )(x2, norm_w, w_in)


def _rotary(x, cos, sin):
    half = QK_DIM // 2
    lane = lax.broadcasted_iota(jnp.int32, x.shape, 1)
    partner = jnp.where(lane % QK_DIM < half, pltpu.roll(x, LANES - half, 1), pltpu.roll(x, half, 1))
    return x * cos + partner * sin


def _tile_rows(tile, size):
    if isinstance(tile, int):
        return pl.ds(tile * size, size)
    return pl.ds(pl.multiple_of(tile * size, size), size)


def _attn_kernel(lamv_ref, sw_ref, tab_ref, q_ref, qn_ref, k_ref, v_ref, g_ref, o_ref,
                 kr_scr, vt_scr, qm_scr, s_scr, acc_scr, bm_scr, mask_scr, *, lam_init, seq):
    tk, tq = ATT_TK, ATT_TQ
    qi = pl.program_id(2)
    n_q = seq // tq
    slot = qi & 1
    nt = (((1,), (1,)), ((), ()))
    lo, hi = slice(0, tk), slice(tk, tq)

    def prep_q(src_ref, tile, dst):
        rows = _tile_rows(tile, tq)
        qf = _rotary(src_ref[0].astype(F32), tab_ref[0, rows, :], tab_ref[1, rows, :]) * (QK_DIM ** -0.5 * LOG2E)
        lane = lax.broadcasted_iota(jnp.int32, (tq, HEAD_W), 1)
        is_c1 = lane < QK_DIM
        qm_scr[dst, 0] = jnp.where(is_c1, qf, 0.0).astype(BF16)
        qm_scr[dst, 1] = jnp.where(is_c1, 0.0, qf).astype(BF16)

    def produce(kt, buf, q_slot, c, cols, masked=False):
        s = lax.dot_general(kr_scr[_tile_rows(kt, tk), :], qm_scr[q_slot, c, cols, :], nt,
                            preferred_element_type=F32)
        if masked:
            s = s + mask_scr[...]
        s_scr[buf, c, :, cols] = s
        return jnp.max(s, axis=0, keepdims=True)

    def consume(kt, buf, c, cols, m_old, bmax, masked=False):
        s = s_scr[buf, c, :, cols]
        if masked:
            s = s + mask_scr[...]
            bmax = jnp.max(s, axis=0, keepdims=True)
        m_new = jnp.maximum(m_old, bmax)
        p = jnp.exp2(s - m_new).astype(BF16)
        acc_scr[c, :, cols] = (jnp.exp2(m_old - m_new) * acc_scr[c, :, cols]
                               + jnp.dot(vt_scr[kt], p, preferred_element_type=F32))
        return m_new

    @pl.when((pl.program_id(0) == 0) & (pl.program_id(1) == 0) & (qi == 0))
    def _():
        row = lax.broadcasted_iota(jnp.int32, (tk, tk), 0)
        col = lax.broadcasted_iota(jnp.int32, (tk, tk), 1)
        mask_scr[...] = jnp.where(row <= col, 0.0, NEG_BIG)

    @pl.when(qi == 0)
    def _():
        def prep_kv(t):
            rows = slice(t * tk, (t + 1) * tk)
            vt_scr[t, :HEAD_W, :] = v_ref[0, rows, :].T
            vt_scr[t, HEAD_W:, :] = jnp.ones((VT_ROWS - HEAD_W, tk), BF16)
            kr_scr[rows, :] = _rotary(k_ref[0, rows, :].astype(F32), tab_ref[0, rows, :],
                                      tab_ref[1, rows, :]).astype(BF16)

        prep_kv(0)
        prep_q(q_ref, 0, 0)
        blocks = [(c, cols) for c in range(2) for cols in (lo, hi)]
        for t in range(1, seq // tk):
            if blocks:
                c, cols = blocks.pop(0)
                bm_scr[c, :, cols] = produce(0, 0, 0, c, cols)
            prep_kv(t)
        for c, cols in blocks:
            bm_scr[c, :, cols] = produce(0, 0, 0, c, cols)

    acc_scr[...] = jnp.zeros_like(acc_scr)

    def produce_consume(kt_p, buf_p, kt_c, buf_c, stats, bmax_c):
        new_stats, new_bmax = [], []
        for i, (c, cols) in enumerate((c, cols) for c in range(2) for cols in (lo, hi)):
            new_bmax.append(produce(kt_p, buf_p, slot, c, cols))
            new_stats.append(consume(kt_c, buf_c, c, cols, stats[i], bmax_c[i]))
        return tuple(new_stats), tuple(new_bmax)

    def pair(j, carry):
        stats, bm_a = carry[:4], carry[4:]
        stats, bm_b = produce_consume(2 * j + 1, 1, 2 * j, 0, stats, bm_a)
        stats, bm_a = produce_consume(2 * j + 2, 0, 2 * j + 1, 1, stats, bm_b)
        return stats + bm_a

    n_full = 2 * qi
    stats0 = (jnp.full((1, tk), NEG_BIG, F32),) * 4
    bm0 = tuple(bm_scr[c, :, cols] for c in range(2) for cols in (lo, hi))
    carry = lax.fori_loop(0, qi, pair, stats0 + bm0)

    def tail(with_next):
        (m1_lo, m1_hi, m2_lo, m2_hi), (_, b1_hi, _, b2_hi) = carry[:4], carry[4:]

        def produce_next(c, cols):
            bm_scr[c, :, cols] = produce(0, 0, 1 - slot, c, cols)

        def epilogue(cols):
            lamv = lamv_ref[...]
            lam = (jnp.exp(jnp.sum(lamv[0:1] * lamv[1:2], axis=-1, keepdims=True))
                   - jnp.exp(jnp.sum(lamv[2:3] * lamv[3:4], axis=-1, keepdims=True)) + lam_init)
            l1 = acc_scr[0, HEAD_W:HEAD_W + 1, cols]
            l2 = acc_scr[1, HEAD_W:HEAD_W + 1, cols]
            d = acc_scr[0, :HEAD_W, cols] * (1.0 / l1) - lam * (acc_scr[1, :HEAD_W, cols] * (1.0 / l2))
            ms = jnp.mean(d * d, axis=0, keepdims=True)
            y = (d * lax.rsqrt(ms + NORM_EPS)).T
            g = g_ref[0, cols, :].astype(F32)
            o_ref[0, cols, :] = (y * (sw_ref[...] * (1.0 - lam_init)) * _silu(g)).astype(BF16)

        b1_sub = produce(n_full + 1, 1, slot, 0, hi, masked=True)
        if with_next:
            prep_q(qn_ref, qi + 1, 1 - slot)
        consume(n_full, 0, 0, lo, m1_lo, None, masked=True)
        b2_sub = produce(n_full + 1, 1, slot, 1, hi, masked=True)
        m1_hi = consume(n_full, 0, 0, hi, m1_hi, b1_hi)
        if with_next:
            produce_next(0, lo)
        consume(n_full, 0, 1, lo, m2_lo, None, masked=True)
        if with_next:
            produce_next(0, hi)
        m2_hi = consume(n_full, 0, 1, hi, m2_hi, b2_hi)
        if with_next:
            produce_next(1, lo)
        epilogue(lo)
        consume(n_full + 1, 1, 0, hi, m1_hi, b1_sub)
        consume(n_full + 1, 1, 1, hi, m2_hi, b2_sub)
        if with_next:
            produce_next(1, hi)
        epilogue(hi)

    pl.when(qi < n_q - 1)(lambda: tail(True))
    pl.when(qi == n_q - 1)(lambda: tail(False))


def _attention(proj3, lamv, subln_w, tables, layer, lam_init):
    b, seq, _ = proj3.shape
    tk, tq = ATT_TK, ATT_TQ
    n_q = seq // tq
    head_blk = lambda off: pl.BlockSpec((1, seq, HEAD_W), lambda bi, h, qi: (bi, 0, off + h))
    tile_blk = lambda off: pl.BlockSpec((1, tq, HEAD_W), lambda bi, h, qi: (bi, qi, off + h))
    return pl.pallas_call(
        functools.partial(_attn_kernel, lam_init=lam_init, seq=seq),
        out_shape=jax.ShapeDtypeStruct((b, seq, A_WIDTH), BF16),
        grid=(b, A_HEADS, n_q),
        in_specs=[
            pl.BlockSpec((None, 4, QK_DIM), lambda bi, h, qi: (layer, 0, 0)),
            pl.BlockSpec((None, 1, HEAD_W), lambda bi, h, qi: (layer, 0, 0)),
            pl.BlockSpec((2, seq, LANES), lambda bi, h, qi: (0, 0, 0)),
            tile_blk(0),
            pl.BlockSpec((1, tq, HEAD_W), lambda bi, h, qi: (bi, jnp.minimum(qi + 1, n_q - 1), h)),
            head_blk(A_HEADS),
            head_blk(2 * A_HEADS),
            tile_blk(3 * A_HEADS),
        ],
        out_specs=pl.BlockSpec((1, tq, HEAD_W), lambda bi, h, qi: (bi, qi, h)),
        scratch_shapes=[
            pltpu.VMEM((seq, HEAD_W), BF16),
            pltpu.VMEM((seq // tk, VT_ROWS, tk), BF16),
            pltpu.VMEM((2, 2, tq, HEAD_W), BF16),
            pltpu.VMEM((2, 2, tk, tq), F32),
            pltpu.VMEM((2, VT_ROWS, tq), F32),
            pltpu.VMEM((2, 1, tq), F32),
            pltpu.VMEM((tk, tk), F32),
        ],
        compiler_params=pltpu.CompilerParams(
            dimension_semantics=("arbitrary", "arbitrary", "arbitrary"), vmem_limit_bytes=VMEM_LIMIT),
        name="diff_attn",
    )(lamv, subln_w, tables, proj3, proj3, proj3, proj3, proj3)


def _mix_kernel(ya_ref, u_ref, vs_ref, gb_ref, xc_ref, bg_ref, cg_ref, gc_ref, xcp_ref, cgp_ref,
                lng_ref, lnb_ref, ws_ref, bs_ref, cw_ref, wout_ref, x_ref, fnw_ref, o_ref,
                cat_scr, z_scr, *, is_last, tiles_per_seq):
    tm = MIX_TM
    n_chunks = tm // B_CHUNK
    i = pl.program_id(0)

    seq_start = (i % tiles_per_seq) == 0
    zp = cgp_ref[...].astype(F32) * xcp_ref[...].astype(F32)
    z_scr[0:HALO, :] = jnp.where(seq_start, 0.0, zp)
    z_scr[HALO:HALO + tm, :] = cg_ref[...].astype(F32) * xc_ref[...].astype(F32)

    tri_row = lax.broadcasted_iota(jnp.int32, (B_CHUNK, B_CHUNK), 0)
    tri_col = lax.broadcasted_iota(jnp.int32, (B_CHUNK, B_CHUNK), 1)
    w_tril = [jnp.where(tri_row >= tri_col, ws_ref[g], 0.0).astype(BF16) for g in range(B_GROUPS)]
    cw = cw_ref[...]

    def mix_b(c):
        rows = slice(c * B_CHUNK, (c + 1) * B_CHUNK)
        v = vs_ref[rows, :].astype(F32)
        mu = jnp.mean(v, axis=-1, keepdims=True)
        vc = v - mu
        var = jnp.mean(vc * vc, axis=-1, keepdims=True)
        vn = (vc * lax.rsqrt(var + LN_EPS) * lng_ref[...] + lnb_ref[...]).astype(BF16)
        for g in range(B_GROUPS):
            cols = slice(g * LANES, (g + 1) * LANES)
            mixed = jnp.dot(w_tril[g], vn[:, cols], preferred_element_type=F32) + bs_ref[g]
            yb = u_ref[rows, cols].astype(F32) * mixed * _silu(gb_ref[rows, cols].astype(F32))
            cat_scr[rows, cols] = yb.astype(BF16)

    def mix_c(c):
        rows = slice(c * B_CHUNK, (c + 1) * B_CHUNK)
        conv = cw[CONV_K - 1:CONV_K] * z_scr[HALO + c * B_CHUNK:HALO + (c + 1) * B_CHUNK, :]
        for tap in range(CONV_K - 1):
            start = HALO + c * B_CHUNK - (CONV_K - 1 - tap)
            conv = conv + cw[tap:tap + 1] * z_scr[start:start + B_CHUNK, :]
        yc = bg_ref[rows, :].astype(F32) * conv * _silu(gc_ref[rows, :].astype(F32))
        cat_scr[rows, B_WIDTH:B_WIDTH + C_WIDTH] = yc.astype(BF16)

    cw_out = D_MODEL // n_chunks
    w_b = slice(A_WIDTH, A_WIDTH + B_WIDTH)
    w_c = slice(A_WIDTH + B_WIDTH, D_MODEL)
    for c in range(n_chunks):
        cols = slice(c * cw_out, (c + 1) * cw_out)
        o_ref[:, cols] = x_ref[:, cols] + jnp.dot(ya_ref[...], wout_ref[0:A_WIDTH, cols],
                                                  preferred_element_type=F32)
        mix_b(c)
    for c in range(n_chunks):
        cols = slice(c * cw_out, (c + 1) * cw_out)
        o_ref[:, cols] += jnp.dot(cat_scr[:, 0:B_WIDTH], wout_ref[w_b, cols], preferred_element_type=F32)
        mix_c(c)
    out = o_ref[...] + jnp.dot(cat_scr[:, B_WIDTH:], wout_ref[w_c, :], preferred_element_type=F32)
    if is_last:
        ms = jnp.mean(out * out, axis=-1, keepdims=True)
        out = out * lax.rsqrt(ms + NORM_EPS) * fnw_ref[...]
    o_ref[...] = out


def _mix_out(ya2, proj2, x2, ln_g, ln_b, w_s, bs_b, conv_w, wout_bf16, layer, fnw, seq, is_last):
    n_rows = x2.shape[0]
    tm = MIX_TM
    base = (4 * A_WIDTH) // B_WIDTH
    pblk = lambda k: pl.BlockSpec((tm, B_WIDTH), lambda i: (i, base + k))
    halo = lambda k: pl.BlockSpec((HALO, B_WIDTH), lambda i: (jnp.maximum(i * (tm // HALO) - 1, 0), base + k))
    per_layer = lambda shape: pl.BlockSpec((None,) + shape, lambda i: (layer,) + (0,) * len(shape))
    return pl.pallas_call(
        functools.partial(_mix_kernel, is_last=is_last, tiles_per_seq=seq // tm),
        out_shape=jax.ShapeDtypeStruct((n_rows, D_MODEL), F32),
        grid=(n_rows // tm,),
        in_specs=[
            pl.BlockSpec((tm, A_WIDTH), lambda i: (i, 0)),
            pblk(0), pblk(1), pblk(2),
            pblk(3), pblk(4), pblk(5), pblk(6),
            halo(3), halo(5),
            per_layer((1, B_WIDTH)), per_layer((1, B_WIDTH)),
            per_layer((B_GROUPS, B_CHUNK, B_CHUNK)),
            per_layer((B_GROUPS, B_CHUNK, LANES)),
            per_layer((CONV_K, C_WIDTH)),
            per_layer((D_MODEL, D_MODEL)),
            pl.BlockSpec((tm, D_MODEL), lambda i: (i, 0)),
            pl.BlockSpec((1, D_MODEL), lambda i: (0, 0)),
        ],
        out_specs=pl.BlockSpec((tm, D_MODEL), lambda i: (i, 0)),
        scratch_shapes=[
            pltpu.VMEM((tm, B_WIDTH + C_WIDTH), BF16),
            pltpu.VMEM((HALO + tm, C_WIDTH), F32),
        ],
        compiler_params=pltpu.CompilerParams(
            dimension_semantics=("arbitrary",), vmem_limit_bytes=VMEM_LIMIT),
        name="mix_out",
    )(ya2, proj2, proj2, proj2, proj2, proj2, proj2, proj2, proj2, proj2,
      ln_g, ln_b, w_s, bs_b, conv_w, wout_bf16, x2, fnw)


def _rotary_tables(seq):
    half = QK_DIM // 2
    pos = jnp.arange(seq, dtype=F32)
    inv_freq = ROPE_THETA ** (-jnp.arange(0, QK_DIM, 2, dtype=F32) / QK_DIM)
    ang = pos[:, None] * inv_freq[None, :]
    cos = jnp.tile(jnp.cos(ang), (1, LANES // half))
    sin = jnp.tile(jnp.sin(ang), (1, LANES // half))
    sign = jnp.where(jnp.arange(LANES) % QK_DIM < half, -1.0, 1.0).astype(F32)
    return jnp.stack([cos, sin * sign])


def kernel(x, norm_w, w_in, lam_q1, lam_k1, lam_q2, lam_k2, subln_w, sgu_ln_g, sgu_ln_b, w_s, b_s,
           conv_w, w_out, final_norm_w):
    b, seq, d = x.shape
    depth = norm_w.shape[0]
    tables = _rotary_tables(seq)
    x2 = x.reshape(b * seq, d)
    fnw = final_norm_w.reshape(1, d)
    norm_w3 = norm_w.reshape(depth, 1, d)
    lamv = jnp.stack([lam_q1, lam_k1, lam_q2, lam_k2], axis=1)
    subln_w3 = subln_w.reshape(depth, 1, HEAD_W)
    ln_g3 = sgu_ln_g.reshape(depth, 1, B_WIDTH)
    ln_b3 = sgu_ln_b.reshape(depth, 1, B_WIDTH)
    bs_b = jnp.broadcast_to(b_s[..., None], (depth, B_GROUPS, B_CHUNK, LANES))
    w_out_bf16 = w_out.astype(BF16)
    for l in range(depth):
        lam_init = 0.8 - 0.6 * math.exp(-0.3 * l)
        proj2 = _project(x2, norm_w3, w_in, l)
        ya = _attention(proj2.reshape(b, seq, PROJ_COLS), lamv, subln_w3, tables, l, lam_init)
        x2 = _mix_out(ya.reshape(b * seq, A_WIDTH), proj2, x2, ln_g3, ln_b3, w_s, bs_b, conv_w,
                      w_out_bf16, l, fnw, seq, l == depth - 1)
    return x2.reshape(b, seq, d)
```

```python
import functools
import math

import jax
import jax.numpy as jnp
from jax import lax
from jax.experimental import pallas as pl
from jax.experimental.pallas import tpu as pltpu

F32 = jnp.float32
BF16 = jnp.bfloat16

D_MODEL = 2048
A_HEADS = 8
QK_DIM = 64
HEAD_W = 2 * QK_DIM
A_WIDTH = A_HEADS * HEAD_W
B_GROUPS = 4
B_CHUNK = 128
B_WIDTH = 512
C_WIDTH = 512
CONV_K = 3
PROJ_COLS = 4 * A_WIDTH + 3 * B_WIDTH + 4 * C_WIDTH
ROPE_THETA = 10000.0
NORM_EPS = 1e-5
LN_EPS = 1e-5
LOG2E = 1.4426950408889634

LANES = 128
VMEM_LIMIT = 56 * 1024 * 1024

PROJ_TM = 2048
PROJ_XB = 1024
PROJ_NORM_STEPS = PROJ_TM // PROJ_XB
PROJ_TN = 512
NORM_ROWS = 128
ATT_TK = 512
ATT_TQ = 2 * ATT_TK
BF16_SUBLANES = 16
VT_ROWS = HEAD_W + BF16_SUBLANES
MIX_TM = 512
HALO = 8

NEG_BIG = -1e30


def _silu(g):
    return g * jax.nn.sigmoid(g)


def _proj_kernel(x_ref, nw_ref, w_ref, o_ref, h_scr):
    j = pl.program_id(1)

    @pl.when(j < PROJ_NORM_STEPS)
    def _():
        nw = nw_ref[...]

        def norm_rows(r, carry):
            x = x_ref[pl.ds(pl.multiple_of(r * NORM_ROWS, NORM_ROWS), NORM_ROWS), :]
            ms = jnp.mean(x * x, axis=-1, keepdims=True)
            dst = pl.ds(pl.multiple_of(j * PROJ_XB + r * NORM_ROWS, NORM_ROWS), NORM_ROWS)
            h_scr[dst, :] = (x * lax.rsqrt(ms + NORM_EPS) * nw).astype(BF16)
            return carry

        lax.fori_loop(0, PROJ_XB // NORM_ROWS, norm_rows, 0)

    @pl.when(j >= PROJ_NORM_STEPS)
    def _():
        o_ref[...] = jnp.dot(h_scr[...], w_ref[...].astype(BF16), preferred_element_type=F32).astype(BF16)


def _project(x2, norm_w, w_in, layer):
    n_rows = x2.shape[0]
    col_tile = lambda j: jnp.maximum(j - PROJ_NORM_STEPS, 0)
    return pl.pallas_call(
        _proj_kernel,
        out_shape=jax.ShapeDtypeStruct((n_rows, PROJ_COLS), BF16),
        grid=(n_rows // PROJ_TM, PROJ_NORM_STEPS + PROJ_COLS // PROJ_TN),
        in_specs=[
            pl.BlockSpec((PROJ_XB, D_MODEL),
                         lambda i, j: (i * PROJ_NORM_STEPS + jnp.minimum(j, PROJ_NORM_STEPS - 1), 0)),
            pl.BlockSpec((None, 1, D_MODEL), lambda i, j: (layer, 0, 0)),
            pl.BlockSpec((None, D_MODEL, PROJ_TN), lambda i, j: (layer, 0, col_tile(j))),
        ],
        out_specs=pl.BlockSpec((PROJ_TM, PROJ_TN), lambda i, j: (i, col_tile(j))),
        scratch_shapes=[pltpu.VMEM((PROJ_TM, D_MODEL), BF16)],
        compiler_params=pltpu.CompilerParams(
            dimension_semantics=("arbitrary", "arbitrary"), vmem_limit_bytes=VMEM_LIMIT),
        name="proj",
    )(x2, norm_w, w_in)


def _rotary(x, cos, sin):
    half = QK_DIM // 2
    lane = lax.broadcasted_iota(jnp.int32, x.shape, 1)
    partner = jnp.where(lane % QK_DIM < half, pltpu.roll(x, LANES - half, 1), pltpu.roll(x, half, 1))
    return x * cos + partner * sin


def _tile_rows(tile, size):
    if isinstance(tile, int):
        return pl.ds(tile * size, size)
    return pl.ds(pl.multiple_of(tile * size, size), size)


def _attn_kernel(lamv_ref, sw_ref, tab_ref, q_ref, k_ref, v_ref, g_ref, o_ref,
                 kr_scr, vt_scr, qm_scr, s_scr, acc_scr, bm_scr, mask_scr, *, lam_init, seq):
    tk, tq = ATT_TK, ATT_TQ
    n_q = seq // tq
    nt = (((1,), (1,)), ((), ()))
    lo, hi = slice(0, tk), slice(tk, tq)

    def prep_q(tile, dst):
        rows = _tile_rows(tile, tq)
        qf = (_rotary(q_ref[0, rows, :].astype(F32), tab_ref[0, rows, :], tab_ref[1, rows, :])
              * (QK_DIM ** -0.5 * LOG2E))
        lane = lax.broadcasted_iota(jnp.int32, (tq, HEAD_W), 1)
        is_c1 = lane < QK_DIM
        qm_scr[dst, 0] = jnp.where(is_c1, qf, 0.0).astype(BF16)
        qm_scr[dst, 1] = jnp.where(is_c1, 0.0, qf).astype(BF16)

    def produce(kt, buf, q_slot, c, cols, masked=False):
        s = lax.dot_general(kr_scr[_tile_rows(kt, tk), :], qm_scr[q_slot, c, cols, :], nt,
                            preferred_element_type=F32)
        if masked:
            s = s + mask_scr[...]
        s_scr[buf, c, :, cols] = s
        return jnp.max(s, axis=0, keepdims=True)

    def consume(kt, buf, c, cols, m_old, bmax, masked=False):
        s = s_scr[buf, c, :, cols]
        if masked:
            s = s + mask_scr[...]
            bmax = jnp.max(s, axis=0, keepdims=True)
        m_new = jnp.maximum(m_old, bmax)
        p = jnp.exp2(s - m_new).astype(BF16)
        acc_scr[c, :, cols] = (jnp.exp2(m_old - m_new) * acc_scr[c, :, cols]
                               + jnp.dot(vt_scr[kt], p, preferred_element_type=F32))
        return m_new

    @pl.when((pl.program_id(0) == 0) & (pl.program_id(1) == 0))
    def _():
        row = lax.broadcasted_iota(jnp.int32, (tk, tk), 0)
        col = lax.broadcasted_iota(jnp.int32, (tk, tk), 1)
        mask_scr[...] = jnp.where(row <= col, 0.0, NEG_BIG)

    def prep_kv(t):
        rows = slice(t * tk, (t + 1) * tk)
        vt_scr[t, :HEAD_W, :] = v_ref[0, rows, :].T
        vt_scr[t, HEAD_W:, :] = jnp.ones((VT_ROWS - HEAD_W, tk), BF16)
        kr_scr[rows, :] = _rotary(k_ref[0, rows, :].astype(F32), tab_ref[0, rows, :],
                                  tab_ref[1, rows, :]).astype(BF16)

    prep_kv(0)
    prep_q(0, 0)
    blocks = [(c, cols) for c in range(2) for cols in (lo, hi)]
    for t in range(1, seq // tk):
        if blocks:
            c, cols = blocks.pop(0)
            bm_scr[c, :, cols] = produce(0, 0, 0, c, cols)
        prep_kv(t)
    for c, cols in blocks:
        bm_scr[c, :, cols] = produce(0, 0, 0, c, cols)

    def q_tile(qi, _):
        slot = qi & 1
        acc_scr[...] = jnp.zeros_like(acc_scr)

        def produce_consume(kt_p, buf_p, kt_c, buf_c, stats, bmax_c):
            new_stats, new_bmax = [], []
            for i, (c, cols) in enumerate((c, cols) for c in range(2) for cols in (lo, hi)):
                new_bmax.append(produce(kt_p, buf_p, slot, c, cols))
                new_stats.append(consume(kt_c, buf_c, c, cols, stats[i], bmax_c[i]))
            return tuple(new_stats), tuple(new_bmax)

        def pair(j, carry):
            stats, bm_a = carry[:4], carry[4:]
            stats, bm_b = produce_consume(2 * j + 1, 1, 2 * j, 0, stats, bm_a)
            stats, bm_a = produce_consume(2 * j + 2, 0, 2 * j + 1, 1, stats, bm_b)
            return stats + bm_a

        n_full = 2 * qi
        stats0 = (jnp.full((1, tk), NEG_BIG, F32),) * 4
        bm0 = tuple(bm_scr[c, :, cols] for c in range(2) for cols in (lo, hi))
        carry = lax.fori_loop(0, qi, pair, stats0 + bm0)

        def tail(with_next):
            (m1_lo, m1_hi, m2_lo, m2_hi), (_, b1_hi, _, b2_hi) = carry[:4], carry[4:]

            def produce_next(c, cols):
                bm_scr[c, :, cols] = produce(0, 0, 1 - slot, c, cols)

            def epilogue(cols):
                rows = pl.ds(pl.multiple_of(qi * tq + cols.start, tk), tk)
                lamv = lamv_ref[...]
                lam = (jnp.exp(jnp.sum(lamv[0:1] * lamv[1:2], axis=-1, keepdims=True))
                       - jnp.exp(jnp.sum(lamv[2:3] * lamv[3:4], axis=-1, keepdims=True)) + lam_init)
                l1 = acc_scr[0, HEAD_W:HEAD_W + 1, cols]
                l2 = acc_scr[1, HEAD_W:HEAD_W + 1, cols]
                d = acc_scr[0, :HEAD_W, cols] * (1.0 / l1) - lam * (acc_scr[1, :HEAD_W, cols] * (1.0 / l2))
                ms = jnp.mean(d * d, axis=0, keepdims=True)
                y = (d * lax.rsqrt(ms + NORM_EPS)).T
                g = g_ref[0, rows, :].astype(F32)
                o_ref[0, rows, :] = (y * (sw_ref[...] * (1.0 - lam_init)) * _silu(g)).astype(BF16)

            b1_sub = produce(n_full + 1, 1, slot, 0, hi, masked=True)
            if with_next:
                prep_q(qi + 1, 1 - slot)
            consume(n_full, 0, 0, lo, m1_lo, None, masked=True)
            b2_sub = produce(n_full + 1, 1, slot, 1, hi, masked=True)
            m1_hi = consume(n_full, 0, 0, hi, m1_hi, b1_hi)
            if with_next:
                produce_next(0, lo)
            consume(n_full, 0, 1, lo, m2_lo, None, masked=True)
            if with_next:
                produce_next(0, hi)
            m2_hi = consume(n_full, 0, 1, hi, m2_hi, b2_hi)
            if with_next:
                produce_next(1, lo)
            epilogue(lo)
            consume(n_full + 1, 1, 0, hi, m1_hi, b1_sub)
            consume(n_full + 1, 1, 1, hi, m2_hi, b2_sub)
            if with_next:
                produce_next(1, hi)
            epilogue(hi)

        pl.when(qi < n_q - 1)(lambda: tail(True))
        pl.when(qi == n_q - 1)(lambda: tail(False))
        return 0

    lax.fori_loop(0, n_q, q_tile, 0)


def _attention(proj3, lamv, subln_w, tables, layer, lam_init):
    b, seq, _ = proj3.shape
    tk, tq = ATT_TK, ATT_TQ
    head_blk = lambda off: pl.BlockSpec((1, seq, HEAD_W), lambda bi, h: (bi, 0, off + h))
    return pl.pallas_call(
        functools.partial(_attn_kernel, lam_init=lam_init, seq=seq),
        out_shape=jax.ShapeDtypeStruct((b, seq, A_WIDTH), BF16),
        grid=(b, A_HEADS),
        in_specs=[
            pl.BlockSpec((None, 4, QK_DIM), lambda bi, h: (layer, 0, 0)),
            pl.BlockSpec((None, 1, HEAD_W), lambda bi, h: (layer, 0, 0)),
            pl.BlockSpec((2, seq, LANES), lambda bi, h: (0, 0, 0)),
            head_blk(0), head_blk(A_HEADS), head_blk(2 * A_HEADS), head_blk(3 * A_HEADS),
        ],
        out_specs=pl.BlockSpec((1, seq, HEAD_W), lambda bi, h: (bi, 0, h)),
        scratch_shapes=[
            pltpu.VMEM((seq, HEAD_W), BF16),
            pltpu.VMEM((seq // tk, VT_ROWS, tk), BF16),
            pltpu.VMEM((2, 2, tq, HEAD_W), BF16),
            pltpu.VMEM((2, 2, tk, tq), F32),
            pltpu.VMEM((2, VT_ROWS, tq), F32),
            pltpu.VMEM((2, 1, tq), F32),
            pltpu.VMEM((tk, tk), F32),
        ],
        compiler_params=pltpu.CompilerParams(
            dimension_semantics=("arbitrary", "arbitrary"), vmem_limit_bytes=VMEM_LIMIT),
        name="diff_attn",
    )(lamv, subln_w, tables, proj3, proj3, proj3, proj3)


def _mix_kernel(ya_ref, u_ref, vs_ref, gb_ref, xc_ref, bg_ref, cg_ref, gc_ref, xcp_ref, cgp_ref,
                lng_ref, lnb_ref, ws_ref, bs_ref, cw_ref, wout_ref, x_ref, fnw_ref, o_ref,
                cat_scr, z_scr, *, is_last, tiles_per_seq):
    tm = MIX_TM
    n_chunks = tm // B_CHUNK
    i = pl.program_id(0)

    seq_start = (i % tiles_per_seq) == 0
    zp = cgp_ref[...].astype(F32) * xcp_ref[...].astype(F32)
    z_scr[0:HALO, :] = jnp.where(seq_start, 0.0, zp)
    z_scr[HALO:HALO + tm, :] = cg_ref[...].astype(F32) * xc_ref[...].astype(F32)

    tri_row = lax.broadcasted_iota(jnp.int32, (B_CHUNK, B_CHUNK), 0)
    tri_col = lax.broadcasted_iota(jnp.int32, (B_CHUNK, B_CHUNK), 1)
    w_tril = [jnp.where(tri_row >= tri_col, ws_ref[g], 0.0).astype(BF16) for g in range(B_GROUPS)]
    cw = cw_ref[...]

    def mix_b(c):
        rows = slice(c * B_CHUNK, (c + 1) * B_CHUNK)
        v = vs_ref[rows, :].astype(F32)
        mu = jnp.mean(v, axis=-1, keepdims=True)
        vc = v - mu
        var = jnp.mean(vc * vc, axis=-1, keepdims=True)
        vn = (vc * lax.rsqrt(var + LN_EPS) * lng_ref[...] + lnb_ref[...]).astype(BF16)
        for g in range(B_GROUPS):
            cols = slice(g * LANES, (g + 1) * LANES)
            mixed = jnp.dot(w_tril[g], vn[:, cols], preferred_element_type=F32) + bs_ref[g]
            yb = u_ref[rows, cols].astype(F32) * mixed * _silu(gb_ref[rows, cols].astype(F32))
            cat_scr[rows, cols] = yb.astype(BF16)

    def mix_c(c):
        rows = slice(c * B_CHUNK, (c + 1) * B_CHUNK)
        conv = cw[CONV_K - 1:CONV_K] * z_scr[HALO + c * B_CHUNK:HALO + (c + 1) * B_CHUNK, :]
        for tap in range(CONV_K - 1):
            start = HALO + c * B_CHUNK - (CONV_K - 1 - tap)
            conv = conv + cw[tap:tap + 1] * z_scr[start:start + B_CHUNK, :]
        yc = bg_ref[rows, :].astype(F32) * conv * _silu(gc_ref[rows, :].astype(F32))
        cat_scr[rows, B_WIDTH:B_WIDTH + C_WIDTH] = yc.astype(BF16)

    cw_out = D_MODEL // n_chunks
    w_b = slice(A_WIDTH, A_WIDTH + B_WIDTH)
    w_c = slice(A_WIDTH + B_WIDTH, D_MODEL)
    for c in range(n_chunks):
        cols = slice(c * cw_out, (c + 1) * cw_out)
        o_ref[:, cols] = x_ref[:, cols] + jnp.dot(ya_ref[...], wout_ref[0:A_WIDTH, cols],
                                                  preferred_element_type=F32)
        mix_b(c)
    for c in range(n_chunks):
        cols = slice(c * cw_out, (c + 1) * cw_out)
        o_ref[:, cols] += jnp.dot(cat_scr[:, 0:B_WIDTH], wout_ref[w_b, cols], preferred_element_type=F32)
        mix_c(c)
    out = o_ref[...] + jnp.dot(cat_scr[:, B_WIDTH:], wout_ref[w_c, :], preferred_element_type=F32)
    if is_last:
        ms = jnp.mean(out * out, axis=-1, keepdims=True)
        out = out * lax.rsqrt(ms + NORM_EPS) * fnw_ref[...]
    o_ref[...] = out


def _mix_out(ya2, proj2, x2, ln_g, ln_b, w_s, bs_b, conv_w, wout_bf16, layer, fnw, seq, is_last):
    n_rows = x2.shape[0]
    tm = MIX_TM
    base = (4 * A_WIDTH) // B_WIDTH
    pblk = lambda k: pl.BlockSpec((tm, B_WIDTH), lambda i: (i, base + k))
    halo = lambda k: pl.BlockSpec((HALO, B_WIDTH), lambda i: (jnp.maximum(i * (tm // HALO) - 1, 0), base + k))
    per_layer = lambda shape: pl.BlockSpec((None,) + shape, lambda i: (layer,) + (0,) * len(shape))
    return pl.pallas_call(
        functools.partial(_mix_kernel, is_last=is_last, tiles_per_seq=seq // tm),
        out_shape=jax.ShapeDtypeStruct((n_rows, D_MODEL), F32),
        grid=(n_rows // tm,),
        in_specs=[
            pl.BlockSpec((tm, A_WIDTH), lambda i: (i, 0)),
            pblk(0), pblk(1), pblk(2),
            pblk(3), pblk(4), pblk(5), pblk(6),
            halo(3), halo(5),
            per_layer((1, B_WIDTH)), per_layer((1, B_WIDTH)),
            per_layer((B_GROUPS, B_CHUNK, B_CHUNK)),
            per_layer((B_GROUPS, B_CHUNK, LANES)),
            per_layer((CONV_K, C_WIDTH)),
            per_layer((D_MODEL, D_MODEL)),
            pl.BlockSpec((tm, D_MODEL), lambda i: (i, 0)),
            pl.BlockSpec((1, D_MODEL), lambda i: (0, 0)),
        ],
        out_specs=pl.BlockSpec((tm, D_MODEL), lambda i: (i, 0)),
        scratch_shapes=[
            pltpu.VMEM((tm, B_WIDTH + C_WIDTH), BF16),
            pltpu.VMEM((HALO + tm, C_WIDTH), F32),
        ],
        compiler_params=pltpu.CompilerParams(
            dimension_semantics=("arbitrary",), vmem_limit_bytes=VMEM_LIMIT),
        name="mix_out",
    )(ya2, proj2, proj2, proj2, proj2, proj2, proj2, proj2, proj2, proj2,
      ln_g, ln_b, w_s, bs_b, conv_w, wout_bf16, x2, fnw)


def _rotary_tables(seq):
    half = QK_DIM // 2
    pos = jnp.arange(seq, dtype=F32)
    inv_freq = ROPE_THETA ** (-jnp.arange(0, QK_DIM, 2, dtype=F32) / QK_DIM)
    ang = pos[:, None] * inv_freq[None, :]
    cos = jnp.tile(jnp.cos(ang), (1, LANES // half))
    sin = jnp.tile(jnp.sin(ang), (1, LANES // half))
    sign = jnp.where(jnp.arange(LANES) % QK_DIM < half, -1.0, 1.0).astype(F32)
    return jnp.stack([cos, sin * sign])


def kernel(x, norm_w, w_in, lam_q1, lam_k1, lam_q2, lam_k2, subln_w, sgu_ln_g, sgu_ln_b, w_s, b_s,
           conv_w, w_out, final_norm_w):
    b, seq, d = x.shape
    depth = norm_w.shape[0]
    tables = _rotary_tables(seq)
    x2 = x.reshape(b * seq, d)
    fnw = final_norm_w.reshape(1, d)
    norm_w3 = norm_w.reshape(depth, 1, d)
    lamv = jnp.stack([lam_q1, lam_k1, lam_q2, lam_k2], axis=1)
    subln_w3 = subln_w.reshape(depth, 1, HEAD_W)
    ln_g3 = sgu_ln_g.reshape(depth, 1, B_WIDTH)
    ln_b3 = sgu_ln_b.reshape(depth, 1, B_WIDTH)
    bs_b = jnp.broadcast_to(b_s[..., None], (depth, B_GROUPS, B_CHUNK, LANES))
    w_out_bf16 = w_out.astype(BF16)
    for l in range(depth):
        lam_init = 0.8 - 0.6 * math.exp(-0.3 * l)
        proj2 = _project(x2, norm_w3, w_in, l)
        ya = _attention(proj2.reshape(b, seq, PROJ_COLS), lamv, subln_w3, tables, l, lam_init)
        x2 = _mix_out(ya.reshape(b * seq, A_WIDTH), proj2, x2, ln_g3, ln_b3, w_s, bs_b, conv_w,
                      w_out_bf16, l, fnw, seq, l == depth - 1)
    return x2.reshape(b, seq, d)
```

```python
import functools
import math

import jax
import jax.numpy as jnp
from jax import lax
from jax.experimental import pallas as pl
from jax.experimental.pallas import tpu as pltpu

F32 = jnp.float32
BF16 = jnp.bfloat16

D_MODEL = 2048
A_HEADS = 8
QK_DIM = 64
HEAD_W = 2 * QK_DIM
A_WIDTH = A_HEADS * HEAD_W
B_GROUPS = 4
B_CHUNK = 128
B_WIDTH = 512
C_WIDTH = 512
CONV_K = 3
PROJ_COLS = 4 * A_WIDTH + 3 * B_WIDTH + 4 * C_WIDTH
ROPE_THETA = 10000.0
NORM_EPS = 1e-5
LN_EPS = 1e-5
LOG2E = 1.4426950408889634

LANES = 128
VMEM_LIMIT = 56 * 1024 * 1024

PROJ_TM = 2048
PROJ_XB = 1024
PROJ_NORM_STEPS = PROJ_TM // PROJ_XB
PROJ_TN = 512
NORM_ROWS = 128
ATT_TK = 512
ATT_TQ = 2 * ATT_TK
BF16_SUBLANES = 16
VT_ROWS = HEAD_W + BF16_SUBLANES
MIX_TM = 512
HALO = 8

NEG_BIG = -1e30


def _silu(g):
    return g * jax.nn.sigmoid(g)


def _proj_kernel(x_ref, nw_ref, w_ref, o_ref, h_scr):
    j = pl.program_id(1)

    @pl.when(j < PROJ_NORM_STEPS)
    def _():
        nw = nw_ref[...]

        def norm_rows(r, carry):
            x = x_ref[pl.ds(pl.multiple_of(r * NORM_ROWS, NORM_ROWS), NORM_ROWS), :]
            ms = jnp.mean(x * x, axis=-1, keepdims=True)
            dst = pl.ds(pl.multiple_of(j * PROJ_XB + r * NORM_ROWS, NORM_ROWS), NORM_ROWS)
            h_scr[dst, :] = (x * lax.rsqrt(ms + NORM_EPS) * nw).astype(BF16)
            return carry

        lax.fori_loop(0, PROJ_XB // NORM_ROWS, norm_rows, 0)

    @pl.when(j >= PROJ_NORM_STEPS)
    def _():
        o_ref[...] = jnp.dot(h_scr[...], w_ref[...].astype(BF16), preferred_element_type=F32).astype(BF16)


def _project(x2, norm_w, w_in, layer):
    n_rows = x2.shape[0]
    col_tile = lambda j: jnp.maximum(j - PROJ_NORM_STEPS, 0)
    return pl.pallas_call(
        _proj_kernel,
        out_shape=jax.ShapeDtypeStruct((n_rows, PROJ_COLS), BF16),
        grid=(n_rows // PROJ_TM, PROJ_NORM_STEPS + PROJ_COLS // PROJ_TN),
        in_specs=[
            pl.BlockSpec((PROJ_XB, D_MODEL),
                         lambda i, j: (i * PROJ_NORM_STEPS + jnp.minimum(j, PROJ_NORM_STEPS - 1), 0)),
            pl.BlockSpec((None, 1, D_MODEL), lambda i, j: (layer, 0, 0)),
            pl.BlockSpec((None, D_MODEL, PROJ_TN), lambda i, j: (layer, 0, col_tile(j))),
        ],
        out_specs=pl.BlockSpec((PROJ_TM, PROJ_TN), lambda i, j: (i, col_tile(j))),
        scratch_shapes=[pltpu.VMEM((PROJ_TM, D_MODEL), BF16)],
        compiler_params=pltpu.CompilerParams(
            dimension_semantics=("arbitrary", "arbitrary"), vmem_limit_bytes=VMEM_LIMIT),
        name="proj",
    )(x2, norm_w, w_in)


def _rotary(x, cos, sin):
    half = QK_DIM // 2
    packed = pltpu.bitcast(x, jnp.uint32)
    lane = lax.broadcasted_iota(jnp.int32, packed.shape, 1)
    partner = jnp.where(lane % QK_DIM < half, pltpu.roll(packed, LANES - half, 1), pltpu.roll(packed, half, 1))
    return x.astype(F32) * cos + pltpu.bitcast(partner, BF16).astype(F32) * sin


def _tile_rows(tile, size):
    if isinstance(tile, int):
        return pl.ds(tile * size, size)
    return pl.ds(pl.multiple_of(tile * size, size), size)


def _attn_kernel(lamv_ref, sw_ref, tab_ref, q_ref, k_ref, v_ref, g_ref, o_ref,
                 kr_scr, vt_scr, qm_scr, s_scr, acc_scr, bm_scr, mask_scr, *, lam_init, seq):
    tk, tq = ATT_TK, ATT_TQ
    n_q = seq // tq
    nt = (((1,), (1,)), ((), ()))
    lo, hi = slice(0, tk), slice(tk, tq)

    def prep_q(tile, dst):
        rows = _tile_rows(tile, tq)
        qf = (_rotary(q_ref[0, rows, :], tab_ref[0, rows, :], tab_ref[1, rows, :])
              * (QK_DIM ** -0.5 * LOG2E))
        lane = lax.broadcasted_iota(jnp.int32, (tq, HEAD_W), 1)
        is_c1 = lane < QK_DIM
        qm_scr[dst, 0] = jnp.where(is_c1, qf, 0.0).astype(BF16)
        qm_scr[dst, 1] = jnp.where(is_c1, 0.0, qf).astype(BF16)

    def produce(kt, buf, q_slot, c, cols, masked=False):
        s = lax.dot_general(kr_scr[_tile_rows(kt, tk), :], qm_scr[q_slot, c, cols, :], nt,
                            preferred_element_type=F32)
        if masked:
            s = s + mask_scr[...]
        s_scr[buf, c, :, cols] = s
        return jnp.max(s, axis=0, keepdims=True)

    def consume(kt, buf, c, cols, m_old, bmax, masked=False):
        s = s_scr[buf, c, :, cols]
        if masked:
            s = s + mask_scr[...]
            bmax = jnp.max(s, axis=0, keepdims=True)
        m_new = jnp.maximum(m_old, bmax)
        p = jnp.exp2(s - m_new).astype(BF16)
        acc_scr[c, :, cols] = (jnp.exp2(m_old - m_new) * acc_scr[c, :, cols]
                               + jnp.dot(vt_scr[kt], p, preferred_element_type=F32))
        return m_new

    @pl.when((pl.program_id(0) == 0) & (pl.program_id(1) == 0))
    def _():
        row = lax.broadcasted_iota(jnp.int32, (tk, tk), 0)
        col = lax.broadcasted_iota(jnp.int32, (tk, tk), 1)
        mask_scr[...] = jnp.where(row <= col, 0.0, NEG_BIG)

    def prep_kv(t):
        rows = slice(t * tk, (t + 1) * tk)
        vt_scr[t, :HEAD_W, :] = v_ref[0, rows, :].T
        vt_scr[t, HEAD_W:, :] = jnp.ones((VT_ROWS - HEAD_W, tk), BF16)
        kr_scr[rows, :] = _rotary(k_ref[0, rows, :], tab_ref[0, rows, :],
                                  tab_ref[1, rows, :]).astype(BF16)

    prep_kv(0)
    prep_q(0, 0)
    blocks = [(c, cols) for c in range(2) for cols in (lo, hi)]
    for t in range(1, seq // tk):
        if blocks:
            c, cols = blocks.pop(0)
            bm_scr[c, :, cols] = produce(0, 0, 0, c, cols)
        prep_kv(t)
    for c, cols in blocks:
        bm_scr[c, :, cols] = produce(0, 0, 0, c, cols)

    def q_tile(qi, _):
        slot = qi & 1
        acc_scr[...] = jnp.zeros_like(acc_scr)

        def produce_consume(kt_p, buf_p, kt_c, buf_c, stats, bmax_c):
            new_stats, new_bmax = [], []
            for i, (c, cols) in enumerate((c, cols) for c in range(2) for cols in (lo, hi)):
                new_bmax.append(produce(kt_p, buf_p, slot, c, cols))
                new_stats.append(consume(kt_c, buf_c, c, cols, stats[i], bmax_c[i]))
            return tuple(new_stats), tuple(new_bmax)

        def pair(j, carry):
            stats, bm_a = carry[:4], carry[4:]
            stats, bm_b = produce_consume(2 * j + 1, 1, 2 * j, 0, stats, bm_a)
            stats, bm_a = produce_consume(2 * j + 2, 0, 2 * j + 1, 1, stats, bm_b)
            return stats + bm_a

        n_full = 2 * qi
        stats0 = (jnp.full((1, tk), NEG_BIG, F32),) * 4
        bm0 = tuple(bm_scr[c, :, cols] for c in range(2) for cols in (lo, hi))
        carry = lax.fori_loop(0, qi // 2, lambda j, cr: pair(2 * j + 1, pair(2 * j, cr)), stats0 + bm0)
        carry = lax.cond(qi % 2 == 1, lambda cr: pair(qi - 1, cr), lambda cr: cr, carry)

        def tail(with_next):
            (m1_lo, m1_hi, m2_lo, m2_hi), (_, b1_hi, _, b2_hi) = carry[:4], carry[4:]

            def produce_next(c, cols):
                bm_scr[c, :, cols] = produce(0, 0, 1 - slot, c, cols)

            def epilogue(cols):
                rows = pl.ds(pl.multiple_of(qi * tq + cols.start, tk), tk)
                lamv = lamv_ref[...]
                lam = (jnp.exp(jnp.sum(lamv[0:1] * lamv[1:2], axis=-1, keepdims=True))
                       - jnp.exp(jnp.sum(lamv[2:3] * lamv[3:4], axis=-1, keepdims=True)) + lam_init)
                l1 = acc_scr[0, HEAD_W:HEAD_W + 1, cols]
                l2 = acc_scr[1, HEAD_W:HEAD_W + 1, cols]
                d = acc_scr[0, :HEAD_W, cols] * (1.0 / l1) - lam * (acc_scr[1, :HEAD_W, cols] * (1.0 / l2))
                ms = jnp.mean(d * d, axis=0, keepdims=True)
                y = (d * lax.rsqrt(ms + NORM_EPS)).T
                g = g_ref[0, rows, :].astype(F32)
                o_ref[0, rows, :] = (y * (sw_ref[...] * (1.0 - lam_init)) * _silu(g)).astype(BF16)

            b1_sub = produce(n_full + 1, 1, slot, 0, hi, masked=True)
            if with_next:
                prep_q(qi + 1, 1 - slot)
            consume(n_full, 0, 0, lo, m1_lo, None, masked=True)
            b2_sub = produce(n_full + 1, 1, slot, 1, hi, masked=True)
            m1_hi = consume(n_full, 0, 0, hi, m1_hi, b1_hi)
            if with_next:
                produce_next(0, lo)
            consume(n_full, 0, 1, lo, m2_lo, None, masked=True)
            if with_next:
                produce_next(0, hi)
            m2_hi = consume(n_full, 0, 1, hi, m2_hi, b2_hi)
            if with_next:
                produce_next(1, lo)
            epilogue(lo)
            consume(n_full + 1, 1, 0, hi, m1_hi, b1_sub)
            consume(n_full + 1, 1, 1, hi, m2_hi, b2_sub)
            if with_next:
                produce_next(1, hi)
            epilogue(hi)

        pl.when(qi < n_q - 1)(lambda: tail(True))
        pl.when(qi == n_q - 1)(lambda: tail(False))
        return 0

    lax.fori_loop(0, n_q, q_tile, 0)


def _attention(proj3, lamv, subln_w, tables, layer, lam_init):
    b, seq, _ = proj3.shape
    tk, tq = ATT_TK, ATT_TQ
    head_blk = lambda off: pl.BlockSpec((1, seq, HEAD_W), lambda bi, h: (bi, 0, off + h))
    return pl.pallas_call(
        functools.partial(_attn_kernel, lam_init=lam_init, seq=seq),
        out_shape=jax.ShapeDtypeStruct((b, seq, A_WIDTH), BF16),
        grid=(b, A_HEADS),
        in_specs=[
            pl.BlockSpec((None, 4, QK_DIM), lambda bi, h: (layer, 0, 0)),
            pl.BlockSpec((None, 1, HEAD_W), lambda bi, h: (layer, 0, 0)),
            pl.BlockSpec((2, seq, LANES), lambda bi, h: (0, 0, 0)),
            head_blk(0), head_blk(A_HEADS), head_blk(2 * A_HEADS), head_blk(3 * A_HEADS),
        ],
        out_specs=pl.BlockSpec((1, seq, HEAD_W), lambda bi, h: (bi, 0, h)),
        scratch_shapes=[
            pltpu.VMEM((seq, HEAD_W), BF16),
            pltpu.VMEM((seq // tk, VT_ROWS, tk), BF16),
            pltpu.VMEM((2, 2, tq, HEAD_W), BF16),
            pltpu.VMEM((2, 2, tk, tq), F32),
            pltpu.VMEM((2, VT_ROWS, tq), F32),
            pltpu.VMEM((2, 1, tq), F32),
            pltpu.VMEM((tk, tk), F32),
        ],
        compiler_params=pltpu.CompilerParams(
            dimension_semantics=("arbitrary", "arbitrary"), vmem_limit_bytes=VMEM_LIMIT),
        name="diff_attn",
    )(lamv, subln_w, tables, proj3, proj3, proj3, proj3)


def _mix_kernel(ya_ref, u_ref, vs_ref, gb_ref, xc_ref, bg_ref, cg_ref, gc_ref, xcp_ref, cgp_ref,
                lng_ref, lnb_ref, ws_ref, bs_ref, cw_ref, wout_ref, x_ref, fnw_ref, o_ref,
                cat_scr, z_scr, *, is_last, tiles_per_seq):
    tm = MIX_TM
    n_chunks = tm // B_CHUNK
    i = pl.program_id(0)

    seq_start = (i % tiles_per_seq) == 0
    zp = cgp_ref[...].astype(F32) * xcp_ref[...].astype(F32)
    z_scr[0:HALO, :] = jnp.where(seq_start, 0.0, zp)
    z_scr[HALO:HALO + tm, :] = cg_ref[...].astype(F32) * xc_ref[...].astype(F32)

    tri_row = lax.broadcasted_iota(jnp.int32, (B_CHUNK, B_CHUNK), 0)
    tri_col = lax.broadcasted_iota(jnp.int32, (B_CHUNK, B_CHUNK), 1)
    w_tril = [jnp.where(tri_row >= tri_col, ws_ref[g], 0.0).astype(BF16) for g in range(B_GROUPS)]
    cw = cw_ref[...]

    def mix_b(c):
        rows = slice(c * B_CHUNK, (c + 1) * B_CHUNK)
        v = vs_ref[rows, :].astype(F32)
        mu = jnp.mean(v, axis=-1, keepdims=True)
        vc = v - mu
        var = jnp.mean(vc * vc, axis=-1, keepdims=True)
        vn = (vc * lax.rsqrt(var + LN_EPS) * lng_ref[...] + lnb_ref[...]).astype(BF16)
        for g in range(B_GROUPS):
            cols = slice(g * LANES, (g + 1) * LANES)
            mixed = jnp.dot(w_tril[g], vn[:, cols], preferred_element_type=F32) + bs_ref[g]
            yb = u_ref[rows, cols].astype(F32) * mixed * _silu(gb_ref[rows, cols].astype(F32))
            cat_scr[rows, cols] = yb.astype(BF16)

    def mix_c(c):
        rows = slice(c * B_CHUNK, (c + 1) * B_CHUNK)
        conv = cw[CONV_K - 1:CONV_K] * z_scr[HALO + c * B_CHUNK:HALO + (c + 1) * B_CHUNK, :]
        for tap in range(CONV_K - 1):
            start = HALO + c * B_CHUNK - (CONV_K - 1 - tap)
            conv = conv + cw[tap:tap + 1] * z_scr[start:start + B_CHUNK, :]
        yc = bg_ref[rows, :].astype(F32) * conv * _silu(gc_ref[rows, :].astype(F32))
        cat_scr[rows, B_WIDTH:B_WIDTH + C_WIDTH] = yc.astype(BF16)

    cw_out = D_MODEL // n_chunks
    w_b = slice(A_WIDTH, A_WIDTH + B_WIDTH)
    w_c = slice(A_WIDTH + B_WIDTH, D_MODEL)
    for c in range(n_chunks):
        cols = slice(c * cw_out, (c + 1) * cw_out)
        o_ref[:, cols] = x_ref[:, cols] + jnp.dot(ya_ref[...], wout_ref[0:A_WIDTH, cols],
                                                  preferred_element_type=F32)
        mix_b(c)
    for c in range(n_chunks):
        cols = slice(c * cw_out, (c + 1) * cw_out)
        o_ref[:, cols] += jnp.dot(cat_scr[:, 0:B_WIDTH], wout_ref[w_b, cols], preferred_element_type=F32)
        mix_c(c)
    out = o_ref[...] + jnp.dot(cat_scr[:, B_WIDTH:], wout_ref[w_c, :], preferred_element_type=F32)
    if is_last:
        ms = jnp.mean(out * out, axis=-1, keepdims=True)
        out = out * lax.rsqrt(ms + NORM_EPS) * fnw_ref[...]
    o_ref[...] = out


def _mix_out(ya2, proj2, x2, ln_g, ln_b, w_s, bs_b, conv_w, wout_bf16, layer, fnw, seq, is_last):
    n_rows = x2.shape[0]
    tm = MIX_TM
    base = (4 * A_WIDTH) // B_WIDTH
    pblk = lambda k: pl.BlockSpec((tm, B_WIDTH), lambda i: (i, base + k))
    halo = lambda k: pl.BlockSpec((HALO, B_WIDTH), lambda i: (jnp.maximum(i * (tm // HALO) - 1, 0), base + k))
    per_layer = lambda shape: pl.BlockSpec((None,) + shape, lambda i: (layer,) + (0,) * len(shape))
    return pl.pallas_call(
        functools.partial(_mix_kernel, is_last=is_last, tiles_per_seq=seq // tm),
        out_shape=jax.ShapeDtypeStruct((n_rows, D_MODEL), F32),
        grid=(n_rows // tm,),
        in_specs=[
            pl.BlockSpec((tm, A_WIDTH), lambda i: (i, 0)),
            pblk(0), pblk(1), pblk(2),
            pblk(3), pblk(4), pblk(5), pblk(6),
            halo(3), halo(5),
            per_layer((1, B_WIDTH)), per_layer((1, B_WIDTH)),
            per_layer((B_GROUPS, B_CHUNK, B_CHUNK)),
            per_layer((B_GROUPS, B_CHUNK, LANES)),
            per_layer((CONV_K, C_WIDTH)),
            per_layer((D_MODEL, D_MODEL)),
            pl.BlockSpec((tm, D_MODEL), lambda i: (i, 0)),
            pl.BlockSpec((1, D_MODEL), lambda i: (0, 0)),
        ],
        out_specs=pl.BlockSpec((tm, D_MODEL), lambda i: (i, 0)),
        scratch_shapes=[
            pltpu.VMEM((tm, B_WIDTH + C_WIDTH), BF16),
            pltpu.VMEM((HALO + tm, C_WIDTH), F32),
        ],
        compiler_params=pltpu.CompilerParams(
            dimension_semantics=("arbitrary",), vmem_limit_bytes=VMEM_LIMIT),
        name="mix_out",
    )(ya2, proj2, proj2, proj2, proj2, proj2, proj2, proj2, proj2, proj2,
      ln_g, ln_b, w_s, bs_b, conv_w, wout_bf16, x2, fnw)


def _rotary_tables(seq):
    half = QK_DIM // 2
    pos = jnp.arange(seq, dtype=F32)
    inv_freq = ROPE_THETA ** (-jnp.arange(0, QK_DIM, 2, dtype=F32) / QK_DIM)
    ang = pos[:, None] * inv_freq[None, :]
    cos = jnp.tile(jnp.cos(ang), (1, LANES // half))
    sin = jnp.tile(jnp.sin(ang), (1, LANES // half))
    sign = jnp.where(jnp.arange(LANES) % QK_DIM < half, -1.0, 1.0).astype(F32)
    return jnp.stack([cos, sin * sign])


def kernel(x, norm_w, w_in, lam_q1, lam_k1, lam_q2, lam_k2, subln_w, sgu_ln_g, sgu_ln_b, w_s, b_s,
           conv_w, w_out, final_norm_w):
    b, seq, d = x.shape
    depth = norm_w.shape[0]
    tables = _rotary_tables(seq)
    x2 = x.reshape(b * seq, d)
    fnw = final_norm_w.reshape(1, d)
    norm_w3 = norm_w.reshape(depth, 1, d)
    lamv = jnp.stack([lam_q1, lam_k1, lam_q2, lam_k2], axis=1)
    subln_w3 = subln_w.reshape(depth, 1, HEAD_W)
    ln_g3 = sgu_ln_g.reshape(depth, 1, B_WIDTH)
    ln_b3 = sgu_ln_b.reshape(depth, 1, B_WIDTH)
    bs_b = jnp.broadcast_to(b_s[..., None], (depth, B_GROUPS, B_CHUNK, LANES))
    w_out_bf16 = w_out.astype(BF16)
    for l in range(depth):
        lam_init = 0.8 - 0.6 * math.exp(-0.3 * l)
        proj2 = _project(x2, norm_w3, w_in, l)
        ya = _attention(proj2.reshape(b, seq, PROJ_COLS), lamv, subln_w3, tables, l, lam_init)
        x2 = _mix_out(ya.reshape(b * seq, A_WIDTH), proj2, x2, ln_g3, ln_b3, w_s, bs_b, conv_w,
                      w_out_bf16, l, fnw, seq, l == depth - 1)
    return x2.reshape(b, seq, d)
```

```python
import functools
import math

import jax
import jax.numpy as jnp
from jax import lax
from jax.experimental import pallas as pl
from jax.experimental.pallas import tpu as pltpu

F32 = jnp.float32
BF16 = jnp.bfloat16

D_MODEL = 2048
A_HEADS = 8
QK_DIM = 64
HEAD_W = 2 * QK_DIM
A_WIDTH = A_HEADS * HEAD_W
B_GROUPS = 4
B_CHUNK = 128
B_WIDTH = 512
C_WIDTH = 512
CONV_K = 3
PROJ_COLS = 4 * A_WIDTH + 3 * B_WIDTH + 4 * C_WIDTH
ROPE_THETA = 10000.0
NORM_EPS = 1e-5
LN_EPS = 1e-5
LOG2E = 1.4426950408889634

LANES = 128
VMEM_LIMIT = 56 * 1024 * 1024

PROJ_TM = 2048
PROJ_XB = 1024
PROJ_NORM_STEPS = PROJ_TM // PROJ_XB
PROJ_TN = 512
NORM_ROWS = 128
ATT_TK = 512
ATT_TQ = 2 * ATT_TK
BF16_SUBLANES = 16
VT_ROWS = HEAD_W + BF16_SUBLANES
MIX_TM = 512
HALO = 8

NEG_BIG = -1e30


def _silu(g):
    return g * jax.nn.sigmoid(g)


def _proj_kernel(x_ref, nw_ref, w_ref, o_ref, h_scr):
    j = pl.program_id(1)

    @pl.when(j < PROJ_NORM_STEPS)
    def _():
        nw = nw_ref[...]

        def norm_rows(r, carry):
            x = x_ref[pl.ds(pl.multiple_of(r * NORM_ROWS, NORM_ROWS), NORM_ROWS), :]
            ms = jnp.mean(x * x, axis=-1, keepdims=True)
            dst = pl.ds(pl.multiple_of(j * PROJ_XB + r * NORM_ROWS, NORM_ROWS), NORM_ROWS)
            h_scr[dst, :] = (x * lax.rsqrt(ms + NORM_EPS) * nw).astype(BF16)
            return carry

        lax.fori_loop(0, PROJ_XB // NORM_ROWS, norm_rows, 0)

    @pl.when(j >= PROJ_NORM_STEPS)
    def _():
        o_ref[...] = jnp.dot(h_scr[...], w_ref[...].astype(BF16), preferred_element_type=F32).astype(BF16)


def _project(x2, norm_w, w_in, layer):
    n_rows = x2.shape[0]
    col_tile = lambda j: jnp.maximum(j - PROJ_NORM_STEPS, 0)
    return pl.pallas_call(
        _proj_kernel,
        out_shape=jax.ShapeDtypeStruct((n_rows, PROJ_COLS), BF16),
        grid=(n_rows // PROJ_TM, PROJ_NORM_STEPS + PROJ_COLS // PROJ_TN),
        in_specs=[
            pl.BlockSpec((PROJ_XB, D_MODEL),
                         lambda i, j: (i * PROJ_NORM_STEPS + jnp.minimum(j, PROJ_NORM_STEPS - 1), 0)),
            pl.BlockSpec((None, 1, D_MODEL), lambda i, j: (layer, 0, 0)),
            pl.BlockSpec((None, D_MODEL, PROJ_TN), lambda i, j: (layer, 0, col_tile(j))),
        ],
        out_specs=pl.BlockSpec((PROJ_TM, PROJ_TN), lambda i, j: (i, col_tile(j))),
        scratch_shapes=[pltpu.VMEM((PROJ_TM, D_MODEL), BF16)],
        compiler_params=pltpu.CompilerParams(
            dimension_semantics=("arbitrary", "arbitrary"), vmem_limit_bytes=VMEM_LIMIT),
        name="proj",
    )(x2, norm_w, w_in)


def _rotary(x, cos, sin):
    half = QK_DIM // 2
    lane = lax.broadcasted_iota(jnp.int32, x.shape, 1)
    partner = jnp.where(lane % QK_DIM < half, pltpu.roll(x, LANES - half, 1), pltpu.roll(x, half, 1))
    return x * cos + partner * sin


def _tile_rows(tile, size):
    if isinstance(tile, int):
        return pl.ds(tile * size, size)
    return pl.ds(pl.multiple_of(tile * size, size), size)


def _attn_kernel(lamv_ref, sw_ref, tab_ref, q_ref, k_ref, v_ref, g_ref, o_ref,
                 kr_scr, vt_scr, qm_scr, s_scr, acc_scr, bm_scr, mask_scr, *, lam_init, seq):
    tk, tq = ATT_TK, ATT_TQ
    n_q = seq // tq
    nt = (((1,), (1,)), ((), ()))
    lo, hi = slice(0, tk), slice(tk, tq)

    def prep_q(tile, dst):
        rows = _tile_rows(tile, tq)
        qf = (_rotary(q_ref[0, rows, :].astype(F32), tab_ref[0, rows, :], tab_ref[1, rows, :])
              * (QK_DIM ** -0.5 * LOG2E))
        lane = lax.broadcasted_iota(jnp.int32, (tq, HEAD_W), 1)
        is_c1 = lane < QK_DIM
        qm_scr[dst, 0] = jnp.where(is_c1, qf, 0.0).astype(BF16)
        qm_scr[dst, 1] = jnp.where(is_c1, 0.0, qf).astype(BF16)

    def produce(kt, buf, q_slot, c, cols, masked=False):
        s = lax.dot_general(kr_scr[_tile_rows(kt, tk), :], qm_scr[q_slot, c, cols, :], nt,
                            preferred_element_type=F32)
        if masked:
            s = s + mask_scr[...]
        s_scr[buf, c, :, cols] = s
        return jnp.max(s, axis=0, keepdims=True)

    def consume(kt, buf, c, cols, m_old, bmax, masked=False):
        s = s_scr[buf, c, :, cols]
        if masked:
            s = s + mask_scr[...]
            bmax = jnp.max(s, axis=0, keepdims=True)
        m_new = jnp.maximum(m_old, bmax)
        p = jnp.exp2(s - m_new).astype(BF16)
        acc_scr[c, :, cols] = (jnp.exp2(m_old - m_new) * acc_scr[c, :, cols]
                               + jnp.dot(vt_scr[kt], p, preferred_element_type=F32))
        return m_new

    @pl.when((pl.program_id(0) == 0) & (pl.program_id(1) == 0))
    def _():
        row = lax.broadcasted_iota(jnp.int32, (tk, tk), 0)
        col = lax.broadcasted_iota(jnp.int32, (tk, tk), 1)
        mask_scr[...] = jnp.where(row <= col, 0.0, NEG_BIG)

    def prep_kv(t):
        rows = slice(t * tk, (t + 1) * tk)
        vt_scr[t, :HEAD_W, :] = v_ref[0, rows, :].T
        vt_scr[t, HEAD_W:, :] = jnp.ones((VT_ROWS - HEAD_W, tk), BF16)
        kr_scr[rows, :] = _rotary(k_ref[0, rows, :].astype(F32), tab_ref[0, rows, :],
                                  tab_ref[1, rows, :]).astype(BF16)

    prep_kv(0)
    prep_q(0, 0)
    blocks = [(c, cols) for c in range(2) for cols in (lo, hi)]
    for t in range(1, seq // tk):
        if blocks:
            c, cols = blocks.pop(0)
            bm_scr[c, :, cols] = produce(0, 0, 0, c, cols)
        prep_kv(t)
    for c, cols in blocks:
        bm_scr[c, :, cols] = produce(0, 0, 0, c, cols)

    def q_tile(qi, _):
        slot = qi & 1
        acc_scr[...] = jnp.zeros_like(acc_scr)

        def produce_consume(kt_p, buf_p, kt_c, buf_c, stats, bmax_c):
            new_stats, new_bmax = [], []
            for i, (c, cols) in enumerate((c, cols) for c in range(2) for cols in (lo, hi)):
                new_bmax.append(produce(kt_p, buf_p, slot, c, cols))
                new_stats.append(consume(kt_c, buf_c, c, cols, stats[i], bmax_c[i]))
            return tuple(new_stats), tuple(new_bmax)

        def pair(j, carry):
            stats, bm_a = carry[:4], carry[4:]
            stats, bm_b = produce_consume(2 * j + 1, 1, 2 * j, 0, stats, bm_a)
            stats, bm_a = produce_consume(2 * j + 2, 0, 2 * j + 1, 1, stats, bm_b)
            return stats + bm_a

        n_full = 2 * qi
        stats0 = (jnp.full((1, tk), NEG_BIG, F32),) * 4
        bm0 = tuple(bm_scr[c, :, cols] for c in range(2) for cols in (lo, hi))
        carry = lax.fori_loop(0, qi // 2, lambda j, cr: pair(2 * j + 1, pair(2 * j, cr)), stats0 + bm0)
        carry = lax.cond(qi % 2 == 1, lambda cr: pair(qi - 1, cr), lambda cr: cr, carry)

        def tail(with_next):
            (m1_lo, m1_hi, m2_lo, m2_hi), (_, b1_hi, _, b2_hi) = carry[:4], carry[4:]

            def produce_next(c, cols):
                bm_scr[c, :, cols] = produce(0, 0, 1 - slot, c, cols)

            def epilogue(cols):
                rows = pl.ds(pl.multiple_of(qi * tq + cols.start, tk), tk)
                lamv = lamv_ref[...]
                lam = (jnp.exp(jnp.sum(lamv[0:1] * lamv[1:2], axis=-1, keepdims=True))
                       - jnp.exp(jnp.sum(lamv[2:3] * lamv[3:4], axis=-1, keepdims=True)) + lam_init)
                l1 = acc_scr[0, HEAD_W:HEAD_W + 1, cols]
                l2 = acc_scr[1, HEAD_W:HEAD_W + 1, cols]
                d = acc_scr[0, :HEAD_W, cols] * (1.0 / l1) - lam * (acc_scr[1, :HEAD_W, cols] * (1.0 / l2))
                ms = jnp.mean(d * d, axis=0, keepdims=True)
                y = (d * lax.rsqrt(ms + NORM_EPS)).T
                g = g_ref[0, rows, :].astype(F32)
                o_ref[0, rows, :] = (y * (sw_ref[...] * (1.0 - lam_init)) * _silu(g)).astype(BF16)

            b1_sub = produce(n_full + 1, 1, slot, 0, hi, masked=True)
            if with_next:
                prep_q(qi + 1, 1 - slot)
            consume(n_full, 0, 0, lo, m1_lo, None, masked=True)
            b2_sub = produce(n_full + 1, 1, slot, 1, hi, masked=True)
            m1_hi = consume(n_full, 0, 0, hi, m1_hi, b1_hi)
            if with_next:
                produce_next(0, lo)
            consume(n_full, 0, 1, lo, m2_lo, None, masked=True)
            if with_next:
                produce_next(0, hi)
            m2_hi = consume(n_full, 0, 1, hi, m2_hi, b2_hi)
            if with_next:
                produce_next(1, lo)
            epilogue(lo)
            consume(n_full + 1, 1, 0, hi, m1_hi, b1_sub)
            consume(n_full + 1, 1, 1, hi, m2_hi, b2_sub)
            if with_next:
                produce_next(1, hi)
            epilogue(hi)

        pl.when(qi < n_q - 1)(lambda: tail(True))
        pl.when(qi == n_q - 1)(lambda: tail(False))
        return 0

    lax.fori_loop(0, n_q, q_tile, 0)


def _attention(proj3, lamv, subln_w, tables, layer, lam_init):
    b, seq, _ = proj3.shape
    tk, tq = ATT_TK, ATT_TQ
    head_blk = lambda off: pl.BlockSpec((1, seq, HEAD_W), lambda bi, h: (bi, 0, off + h))
    return pl.pallas_call(
        functools.partial(_attn_kernel, lam_init=lam_init, seq=seq),
        out_shape=jax.ShapeDtypeStruct((b, seq, A_WIDTH), BF16),
        grid=(b, A_HEADS),
        in_specs=[
            pl.BlockSpec((None, 4, QK_DIM), lambda bi, h: (layer, 0, 0)),
            pl.BlockSpec((None, 1, HEAD_W), lambda bi, h: (layer, 0, 0)),
            pl.BlockSpec((2, seq, LANES), lambda bi, h: (0, 0, 0)),
            head_blk(0), head_blk(A_HEADS), head_blk(2 * A_HEADS), head_blk(3 * A_HEADS),
        ],
        out_specs=pl.BlockSpec((1, seq, HEAD_W), lambda bi, h: (bi, 0, h)),
        scratch_shapes=[
            pltpu.VMEM((seq, HEAD_W), BF16),
            pltpu.VMEM((seq // tk, VT_ROWS, tk), BF16),
            pltpu.VMEM((2, 2, tq, HEAD_W), BF16),
            pltpu.VMEM((2, 2, tk, tq), F32),
            pltpu.VMEM((2, VT_ROWS, tq), F32),
            pltpu.VMEM((2, 1, tq), F32),
            pltpu.VMEM((tk, tk), F32),
        ],
        compiler_params=pltpu.CompilerParams(
            dimension_semantics=("arbitrary", "arbitrary"), vmem_limit_bytes=VMEM_LIMIT),
        name="diff_attn",
    )(lamv, subln_w, tables, proj3, proj3, proj3, proj3)


def _mix_kernel(ya_ref, u_ref, vs_ref, gb_ref, xc_ref, bg_ref, cg_ref, gc_ref, xcp_ref, cgp_ref,
                lng_ref, lnb_ref, ws_ref, bs_ref, cw_ref, wout_ref, x_ref, fnw_ref, o_ref,
                cat_scr, z_scr, *, is_last, tiles_per_seq):
    tm = MIX_TM
    n_chunks = tm // B_CHUNK
    i = pl.program_id(0)

    seq_start = (i % tiles_per_seq) == 0
    zp = cgp_ref[...].astype(F32) * xcp_ref[...].astype(F32)
    z_scr[0:HALO, :] = jnp.where(seq_start, 0.0, zp)
    z_scr[HALO:HALO + tm, :] = cg_ref[...].astype(F32) * xc_ref[...].astype(F32)

    tri_row = lax.broadcasted_iota(jnp.int32, (B_CHUNK, B_CHUNK), 0)
    tri_col = lax.broadcasted_iota(jnp.int32, (B_CHUNK, B_CHUNK), 1)
    w_tril = [jnp.where(tri_row >= tri_col, ws_ref[g], 0.0).astype(BF16) for g in range(B_GROUPS)]
    cw = cw_ref[...]

    def mix_b(c):
        rows = slice(c * B_CHUNK, (c + 1) * B_CHUNK)
        v = vs_ref[rows, :].astype(F32)
        mu = jnp.mean(v, axis=-1, keepdims=True)
        vc = v - mu
        var = jnp.mean(vc * vc, axis=-1, keepdims=True)
        vn = (vc * lax.rsqrt(var + LN_EPS) * lng_ref[...] + lnb_ref[...]).astype(BF16)
        for g in range(B_GROUPS):
            cols = slice(g * LANES, (g + 1) * LANES)
            mixed = jnp.dot(w_tril[g], vn[:, cols], preferred_element_type=F32) + bs_ref[g]
            yb = u_ref[rows, cols].astype(F32) * mixed * _silu(gb_ref[rows, cols].astype(F32))
            cat_scr[rows, cols] = yb.astype(BF16)

    def mix_c(c):
        rows = slice(c * B_CHUNK, (c + 1) * B_CHUNK)
        conv = cw[CONV_K - 1:CONV_K] * z_scr[HALO + c * B_CHUNK:HALO + (c + 1) * B_CHUNK, :]
        for tap in range(CONV_K - 1):
            start = HALO + c * B_CHUNK - (CONV_K - 1 - tap)
            conv = conv + cw[tap:tap + 1] * z_scr[start:start + B_CHUNK, :]
        yc = bg_ref[rows, :].astype(F32) * conv * _silu(gc_ref[rows, :].astype(F32))
        cat_scr[rows, B_WIDTH:B_WIDTH + C_WIDTH] = yc.astype(BF16)

    cw_out = D_MODEL // n_chunks
    w_b = slice(A_WIDTH, A_WIDTH + B_WIDTH)
    w_c = slice(A_WIDTH + B_WIDTH, D_MODEL)
    for c in range(n_chunks):
        cols = slice(c * cw_out, (c + 1) * cw_out)
        o_ref[:, cols] = x_ref[:, cols] + jnp.dot(ya_ref[...], wout_ref[0:A_WIDTH, cols],
                                                  preferred_element_type=F32)
        mix_b(c)
    for c in range(n_chunks):
        cols = slice(c * cw_out, (c + 1) * cw_out)
        o_ref[:, cols] += jnp.dot(cat_scr[:, 0:B_WIDTH], wout_ref[w_b, cols], preferred_element_type=F32)
        mix_c(c)
    out = o_ref[...] + jnp.dot(cat_scr[:, B_WIDTH:], wout_ref[w_c, :], preferred_element_type=F32)
    if is_last:
        ms = jnp.mean(out * out, axis=-1, keepdims=True)
        out = out * lax.rsqrt(ms + NORM_EPS) * fnw_ref[...]
    o_ref[...] = out


def _mix_out(ya2, proj2, x2, ln_g, ln_b, w_s, bs_b, conv_w, wout_bf16, layer, fnw, seq, is_last):
    n_rows = x2.shape[0]
    tm = MIX_TM
    base = (4 * A_WIDTH) // B_WIDTH
    pblk = lambda k: pl.BlockSpec((tm, B_WIDTH), lambda i: (i, base + k))
    halo = lambda k: pl.BlockSpec((HALO, B_WIDTH), lambda i: (jnp.maximum(i * (tm // HALO) - 1, 0), base + k))
    per_layer = lambda shape: pl.BlockSpec((None,) + shape, lambda i: (layer,) + (0,) * len(shape))
    return pl.pallas_call(
        functools.partial(_mix_kernel, is_last=is_last, tiles_per_seq=seq // tm),
        out_shape=jax.ShapeDtypeStruct((n_rows, D_MODEL), F32),
        grid=(n_rows // tm,),
        in_specs=[
            pl.BlockSpec((tm, A_WIDTH), lambda i: (i, 0)),
            pblk(0), pblk(1), pblk(2),
            pblk(3), pblk(4), pblk(5), pblk(6),
            halo(3), halo(5),
            per_layer((1, B_WIDTH)), per_layer((1, B_WIDTH)),
            per_layer((B_GROUPS, B_CHUNK, B_CHUNK)),
            per_layer((B_GROUPS, B_CHUNK, LANES)),
            per_layer((CONV_K, C_WIDTH)),
            per_layer((D_MODEL, D_MODEL)),
            pl.BlockSpec((tm, D_MODEL), lambda i: (i, 0)),
            pl.BlockSpec((1, D_MODEL), lambda i: (0, 0)),
        ],
        out_specs=pl.BlockSpec((tm, D_MODEL), lambda i: (i, 0)),
        scratch_shapes=[
            pltpu.VMEM((tm, B_WIDTH + C_WIDTH), BF16),
            pltpu.VMEM((HALO + tm, C_WIDTH), F32),
        ],
        compiler_params=pltpu.CompilerParams(
            dimension_semantics=("arbitrary",), vmem_limit_bytes=VMEM_LIMIT),
        name="mix_out",
    )(ya2, proj2, proj2, proj2, proj2, proj2, proj2, proj2, proj2, proj2,
      ln_g, ln_b, w_s, bs_b, conv_w, wout_bf16, x2, fnw)


def _rotary_tables(seq):
    half = QK_DIM // 2
    pos = jnp.arange(seq, dtype=F32)
    inv_freq = ROPE_THETA ** (-jnp.arange(0, QK_DIM, 2, dtype=F32) / QK_DIM)
    ang = pos[:, None] * inv_freq[None, :]
    cos = jnp.tile(jnp.cos(ang), (1, LANES // half))
    sin = jnp.tile(jnp.sin(ang), (1, LANES // half))
    sign = jnp.where(jnp.arange(LANES) % QK_DIM < half, -1.0, 1.0).astype(F32)
    return jnp.stack([cos, sin * sign])


def kernel(x, norm_w, w_in, lam_q1, lam_k1, lam_q2, lam_k2, subln_w, sgu_ln_g, sgu_ln_b, w_s, b_s,
           conv_w, w_out, final_norm_w):
    b, seq, d = x.shape
    depth = norm_w.shape[0]
    tables = _rotary_tables(seq)
    x2 = x.reshape(b * seq, d)
    fnw = final_norm_w.reshape(1, d)
    norm_w3 = norm_w.reshape(depth, 1, d)
    lamv = jnp.stack([lam_q1, lam_k1, lam_q2, lam_k2], axis=1)
    subln_w3 = subln_w.reshape(depth, 1, HEAD_W)
    ln_g3 = sgu_ln_g.reshape(depth, 1, B_WIDTH)
    ln_b3 = sgu_ln_b.reshape(depth, 1, B_WIDTH)
    bs_b = jnp.broadcast_to(b_s[..., None], (depth, B_GROUPS, B_CHUNK, LANES))
    w_out_bf16 = w_out.astype(BF16)
    for l in range(depth):
        lam_init = 0.8 - 0.6 * math.exp(-0.3 * l)
        proj2 = _project(x2, norm_w3, w_in, l)
        ya = _attention(proj2.reshape(b, seq, PROJ_COLS), lamv, subln_w3, tables, l, lam_init)
        x2 = _mix_out(ya.reshape(b * seq, A_WIDTH), proj2, x2, ln_g3, ln_b3, w_s, bs_b, conv_w,
                      w_out_bf16, l, fnw, seq, l == depth - 1)
    return x2.reshape(b, seq, d)
```

```python
import functools
import math

import jax
import jax.numpy as jnp
from jax import lax
from jax.experimental import pallas as pl
from jax.experimental.pallas import tpu as pltpu

F32 = jnp.float32
BF16 = jnp.bfloat16

D_MODEL = 2048
A_HEADS = 8
QK_DIM = 64
HEAD_W = 2 * QK_DIM
A_WIDTH = A_HEADS * HEAD_W
B_GROUPS = 4
B_CHUNK = 128
B_WIDTH = 512
C_WIDTH = 512
CONV_K = 3
PROJ_COLS = 4 * A_WIDTH + 3 * B_WIDTH + 4 * C_WIDTH
ROPE_THETA = 10000.0
NORM_EPS = 1e-5
LN_EPS = 1e-5
LOG2E = 1.4426950408889634

LANES = 128
VMEM_LIMIT = 56 * 1024 * 1024

PROJ_TM = 2048
PROJ_XB = 1024
PROJ_NORM_STEPS = PROJ_TM // PROJ_XB
PROJ_TN = 512
NORM_ROWS = 128
ATT_TK = 512
ATT_TQ = 2 * ATT_TK
BF16_SUBLANES = 16
VT_ROWS = HEAD_W + BF16_SUBLANES
MIX_TM = 512
HALO = 8

NEG_BIG = -1e30


def _silu(g):
    return g * jax.nn.sigmoid(g)


def _proj_kernel(x_ref, nw_ref, w_ref, o_ref, h_scr):
    j = pl.program_id(1)

    @pl.when(j < PROJ_NORM_STEPS)
    def _():
        nw = nw_ref[...]

        def norm_rows(r, carry):
            x = x_ref[pl.ds(pl.multiple_of(r * NORM_ROWS, NORM_ROWS), NORM_ROWS), :]
            ms = jnp.mean(x * x, axis=-1, keepdims=True)
            dst = pl.ds(pl.multiple_of(j * PROJ_XB + r * NORM_ROWS, NORM_ROWS), NORM_ROWS)
            h_scr[dst, :] = (x * lax.rsqrt(ms + NORM_EPS) * nw).astype(BF16)
            return carry

        lax.fori_loop(0, PROJ_XB // NORM_ROWS, norm_rows, 0)

    @pl.when(j >= PROJ_NORM_STEPS)
    def _():
        o_ref[...] = jnp.dot(h_scr[...], w_ref[...].astype(BF16), preferred_element_type=F32).astype(BF16)


def _project(x2, norm_w, w_in, layer):
    n_rows = x2.shape[0]
    col_tile = lambda j: jnp.maximum(j - PROJ_NORM_STEPS, 0)
    return pl.pallas_call(
        _proj_kernel,
        out_shape=jax.ShapeDtypeStruct((n_rows, PROJ_COLS), BF16),
        grid=(n_rows // PROJ_TM, PROJ_NORM_STEPS + PROJ_COLS // PROJ_TN),
        in_specs=[
            pl.BlockSpec((PROJ_XB, D_MODEL),
                         lambda i, j: (i * PROJ_NORM_STEPS + jnp.minimum(j, PROJ_NORM_STEPS - 1), 0)),
            pl.BlockSpec((None, 1, D_MODEL), lambda i, j: (layer, 0, 0)),
            pl.BlockSpec((None, D_MODEL, PROJ_TN), lambda i, j: (layer, 0, col_tile(j))),
        ],
        out_specs=pl.BlockSpec((PROJ_TM, PROJ_TN), lambda i, j: (i, col_tile(j))),
        scratch_shapes=[pltpu.VMEM((PROJ_TM, D_MODEL), BF16)],
        compiler_params=pltpu.CompilerParams(
            dimension_semantics=("arbitrary", "arbitrary"), vmem_limit_bytes=VMEM_LIMIT),
        name="proj",
    )(x2, norm_w, w_in)


def _rotary(x, swap, cos, sin):
    partner = jnp.dot(x, swap, preferred_element_type=F32)
    return x.astype(F32) * cos + partner * sin


def _tile_rows(tile, size):
    if isinstance(tile, int):
        return pl.ds(tile * size, size)
    return pl.ds(pl.multiple_of(tile * size, size), size)


def _attn_kernel(lamv_ref, sw_ref, tab_ref, q_ref, k_ref, v_ref, g_ref, o_ref,
                 kr_scr, vt_scr, qm_scr, s_scr, acc_scr, bm_scr, mask_scr, swap_scr, *, lam_init, seq):
    tk, tq = ATT_TK, ATT_TQ
    n_q = seq // tq
    nt = (((1,), (1,)), ((), ()))
    lo, hi = slice(0, tk), slice(tk, tq)

    def prep_q(tile, dst):
        rows = _tile_rows(tile, tq)
        qf = (_rotary(q_ref[0, rows, :], swap_scr[...], tab_ref[0, rows, :], tab_ref[1, rows, :])
              * (QK_DIM ** -0.5 * LOG2E))
        lane = lax.broadcasted_iota(jnp.int32, (tq, HEAD_W), 1)
        is_c1 = lane < QK_DIM
        qm_scr[dst, 0] = jnp.where(is_c1, qf, 0.0).astype(BF16)
        qm_scr[dst, 1] = jnp.where(is_c1, 0.0, qf).astype(BF16)

    def produce(kt, buf, q_slot, c, cols, masked=False):
        s = lax.dot_general(kr_scr[_tile_rows(kt, tk), :], qm_scr[q_slot, c, cols, :], nt,
                            preferred_element_type=F32)
        if masked:
            s = s + mask_scr[...]
        s_scr[buf, c, :, cols] = s
        return jnp.max(s, axis=0, keepdims=True)

    def consume(kt, buf, c, cols, m_old, bmax, masked=False):
        s = s_scr[buf, c, :, cols]
        if masked:
            s = s + mask_scr[...]
            bmax = jnp.max(s, axis=0, keepdims=True)
        m_new = jnp.maximum(m_old, bmax)
        p = jnp.exp2(s - m_new).astype(BF16)
        acc_scr[c, :, cols] = (jnp.exp2(m_old - m_new) * acc_scr[c, :, cols]
                               + jnp.dot(vt_scr[kt], p, preferred_element_type=F32))
        return m_new

    @pl.when((pl.program_id(0) == 0) & (pl.program_id(1) == 0))
    def _():
        row = lax.broadcasted_iota(jnp.int32, (tk, tk), 0)
        col = lax.broadcasted_iota(jnp.int32, (tk, tk), 1)
        mask_scr[...] = jnp.where(row <= col, 0.0, NEG_BIG)
        src = lax.broadcasted_iota(jnp.int32, (HEAD_W, HEAD_W), 0)
        dst = lax.broadcasted_iota(jnp.int32, (HEAD_W, HEAD_W), 1)
        swap_scr[...] = jnp.where(src == (dst ^ (QK_DIM // 2)), 1.0, 0.0).astype(BF16)

    def prep_kv(t):
        rows = slice(t * tk, (t + 1) * tk)
        vt_scr[t, :HEAD_W, :] = v_ref[0, rows, :].T
        vt_scr[t, HEAD_W:, :] = jnp.ones((VT_ROWS - HEAD_W, tk), BF16)
        kr_scr[rows, :] = _rotary(k_ref[0, rows, :], swap_scr[...], tab_ref[0, rows, :],
                                  tab_ref[1, rows, :]).astype(BF16)

    prep_kv(0)
    prep_q(0, 0)
    blocks = [(c, cols) for c in range(2) for cols in (lo, hi)]
    for t in range(1, seq // tk):
        if blocks:
            c, cols = blocks.pop(0)
            bm_scr[c, :, cols] = produce(0, 0, 0, c, cols)
        prep_kv(t)
    for c, cols in blocks:
        bm_scr[c, :, cols] = produce(0, 0, 0, c, cols)

    def q_tile(qi, _):
        slot = qi & 1
        acc_scr[...] = jnp.zeros_like(acc_scr)

        def produce_consume(kt_p, buf_p, kt_c, buf_c, stats, bmax_c):
            new_stats, new_bmax = [], []
            for i, (c, cols) in enumerate((c, cols) for c in range(2) for cols in (lo, hi)):
                new_bmax.append(produce(kt_p, buf_p, slot, c, cols))
                new_stats.append(consume(kt_c, buf_c, c, cols, stats[i], bmax_c[i]))
            return tuple(new_stats), tuple(new_bmax)

        def pair(j, carry):
            stats, bm_a = carry[:4], carry[4:]
            stats, bm_b = produce_consume(2 * j + 1, 1, 2 * j, 0, stats, bm_a)
            stats, bm_a = produce_consume(2 * j + 2, 0, 2 * j + 1, 1, stats, bm_b)
            return stats + bm_a

        n_full = 2 * qi
        stats0 = (jnp.full((1, tk), NEG_BIG, F32),) * 4
        bm0 = tuple(bm_scr[c, :, cols] for c in range(2) for cols in (lo, hi))
        carry = lax.fori_loop(0, qi // 2, lambda j, cr: pair(2 * j + 1, pair(2 * j, cr)), stats0 + bm0)
        carry = lax.cond(qi % 2 == 1, lambda cr: pair(qi - 1, cr), lambda cr: cr, carry)

        def tail(with_next):
            (m1_lo, m1_hi, m2_lo, m2_hi), (_, b1_hi, _, b2_hi) = carry[:4], carry[4:]

            def produce_next(c, cols):
                bm_scr[c, :, cols] = produce(0, 0, 1 - slot, c, cols)

            def epilogue(cols):
                rows = pl.ds(pl.multiple_of(qi * tq + cols.start, tk), tk)
                lamv = lamv_ref[...]
                lam = (jnp.exp(jnp.sum(lamv[0:1] * lamv[1:2], axis=-1, keepdims=True))
                       - jnp.exp(jnp.sum(lamv[2:3] * lamv[3:4], axis=-1, keepdims=True)) + lam_init)
                l1 = acc_scr[0, HEAD_W:HEAD_W + 1, cols]
                l2 = acc_scr[1, HEAD_W:HEAD_W + 1, cols]
                d = acc_scr[0, :HEAD_W, cols] * (1.0 / l1) - lam * (acc_scr[1, :HEAD_W, cols] * (1.0 / l2))
                ms = jnp.mean(d * d, axis=0, keepdims=True)
                y = (d * lax.rsqrt(ms + NORM_EPS)).T
                g = g_ref[0, rows, :].astype(F32)
                o_ref[0, rows, :] = (y * (sw_ref[...] * (1.0 - lam_init)) * _silu(g)).astype(BF16)

            b1_sub = produce(n_full + 1, 1, slot, 0, hi, masked=True)
            if with_next:
                prep_q(qi + 1, 1 - slot)
            consume(n_full, 0, 0, lo, m1_lo, None, masked=True)
            b2_sub = produce(n_full + 1, 1, slot, 1, hi, masked=True)
            m1_hi = consume(n_full, 0, 0, hi, m1_hi, b1_hi)
            if with_next:
                produce_next(0, lo)
            consume(n_full, 0, 1, lo, m2_lo, None, masked=True)
            if with_next:
                produce_next(0, hi)
            m2_hi = consume(n_full, 0, 1, hi, m2_hi, b2_hi)
            if with_next:
                produce_next(1, lo)
            epilogue(lo)
            consume(n_full + 1, 1, 0, hi, m1_hi, b1_sub)
            consume(n_full + 1, 1, 1, hi, m2_hi, b2_sub)
            if with_next:
                produce_next(1, hi)
            epilogue(hi)

        pl.when(qi < n_q - 1)(lambda: tail(True))
        pl.when(qi == n_q - 1)(lambda: tail(False))
        return 0

    lax.fori_loop(0, n_q, q_tile, 0)


def _attention(proj3, lamv, subln_w, tables, layer, lam_init):
    b, seq, _ = proj3.shape
    tk, tq = ATT_TK, ATT_TQ
    head_blk = lambda off: pl.BlockSpec((1, seq, HEAD_W), lambda bi, h: (bi, 0, off + h))
    return pl.pallas_call(
        functools.partial(_attn_kernel, lam_init=lam_init, seq=seq),
        out_shape=jax.ShapeDtypeStruct((b, seq, A_WIDTH), BF16),
        grid=(b, A_HEADS),
        in_specs=[
            pl.BlockSpec((None, 4, QK_DIM), lambda bi, h: (layer, 0, 0)),
            pl.BlockSpec((None, 1, HEAD_W), lambda bi, h: (layer, 0, 0)),
            pl.BlockSpec((2, seq, LANES), lambda bi, h: (0, 0, 0)),
            head_blk(0), head_blk(A_HEADS), head_blk(2 * A_HEADS), head_blk(3 * A_HEADS),
        ],
        out_specs=pl.BlockSpec((1, seq, HEAD_W), lambda bi, h: (bi, 0, h)),
        scratch_shapes=[
            pltpu.VMEM((seq, HEAD_W), BF16),
            pltpu.VMEM((seq // tk, VT_ROWS, tk), BF16),
            pltpu.VMEM((2, 2, tq, HEAD_W), BF16),
            pltpu.VMEM((2, 2, tk, tq), F32),
            pltpu.VMEM((2, VT_ROWS, tq), F32),
            pltpu.VMEM((2, 1, tq), F32),
            pltpu.VMEM((tk, tk), F32),
            pltpu.VMEM((HEAD_W, HEAD_W), BF16),
        ],
        compiler_params=pltpu.CompilerParams(
            dimension_semantics=("arbitrary", "arbitrary"), vmem_limit_bytes=VMEM_LIMIT),
        name="diff_attn",
    )(lamv, subln_w, tables, proj3, proj3, proj3, proj3)


def _mix_kernel(ya_ref, u_ref, vs_ref, gb_ref, xc_ref, bg_ref, cg_ref, gc_ref, xcp_ref, cgp_ref,
                lng_ref, lnb_ref, ws_ref, bs_ref, cw_ref, wout_ref, x_ref, fnw_ref, o_ref,
                cat_scr, z_scr, *, is_last, tiles_per_seq):
    tm = MIX_TM
    n_chunks = tm // B_CHUNK
    i = pl.program_id(0)

    seq_start = (i % tiles_per_seq) == 0
    zp = cgp_ref[...].astype(F32) * xcp_ref[...].astype(F32)
    z_scr[0:HALO, :] = jnp.where(seq_start, 0.0, zp)
    z_scr[HALO:HALO + tm, :] = cg_ref[...].astype(F32) * xc_ref[...].astype(F32)

    tri_row = lax.broadcasted_iota(jnp.int32, (B_CHUNK, B_CHUNK), 0)
    tri_col = lax.broadcasted_iota(jnp.int32, (B_CHUNK, B_CHUNK), 1)
    w_tril = [jnp.where(tri_row >= tri_col, ws_ref[g], 0.0).astype(BF16) for g in range(B_GROUPS)]
    cw = cw_ref[...]

    def mix_b(c):
        rows = slice(c * B_CHUNK, (c + 1) * B_CHUNK)
        v = vs_ref[rows, :].astype(F32)
        mu = jnp.mean(v, axis=-1, keepdims=True)
        vc = v - mu
        var = jnp.mean(vc * vc, axis=-1, keepdims=True)
        vn = (vc * lax.rsqrt(var + LN_EPS) * lng_ref[...] + lnb_ref[...]).astype(BF16)
        for g in range(B_GROUPS):
            cols = slice(g * LANES, (g + 1) * LANES)
            mixed = jnp.dot(w_tril[g], vn[:, cols], preferred_element_type=F32) + bs_ref[g]
            yb = u_ref[rows, cols].astype(F32) * mixed * _silu(gb_ref[rows, cols].astype(F32))
            cat_scr[rows, cols] = yb.astype(BF16)

    def mix_c(c):
        rows = slice(c * B_CHUNK, (c + 1) * B_CHUNK)
        conv = cw[CONV_K - 1:CONV_K] * z_scr[HALO + c * B_CHUNK:HALO + (c + 1) * B_CHUNK, :]
        for tap in range(CONV_K - 1):
            start = HALO + c * B_CHUNK - (CONV_K - 1 - tap)
            conv = conv + cw[tap:tap + 1] * z_scr[start:start + B_CHUNK, :]
        yc = bg_ref[rows, :].astype(F32) * conv * _silu(gc_ref[rows, :].astype(F32))
        cat_scr[rows, B_WIDTH:B_WIDTH + C_WIDTH] = yc.astype(BF16)

    cw_out = D_MODEL // n_chunks
    w_b = slice(A_WIDTH, A_WIDTH + B_WIDTH)
    w_c = slice(A_WIDTH + B_WIDTH, D_MODEL)
    for c in range(n_chunks):
        cols = slice(c * cw_out, (c + 1) * cw_out)
        o_ref[:, cols] = x_ref[:, cols] + jnp.dot(ya_ref[...], wout_ref[0:A_WIDTH, cols],
                                                  preferred_element_type=F32)
        mix_b(c)
    for c in range(n_chunks):
        cols = slice(c * cw_out, (c + 1) * cw_out)
        o_ref[:, cols] += jnp.dot(cat_scr[:, 0:B_WIDTH], wout_ref[w_b, cols], preferred_element_type=F32)
        mix_c(c)
    out = o_ref[...] + jnp.dot(cat_scr[:, B_WIDTH:], wout_ref[w_c, :], preferred_element_type=F32)
    if is_last:
        ms = jnp.mean(out * out, axis=-1, keepdims=True)
        out = out * lax.rsqrt(ms + NORM_EPS) * fnw_ref[...]
    o_ref[...] = out


def _mix_out(ya2, proj2, x2, ln_g, ln_b, w_s, bs_b, conv_w, wout_bf16, layer, fnw, seq, is_last):
    n_rows = x2.shape[0]
    tm = MIX_TM
    base = (4 * A_WIDTH) // B_WIDTH
    pblk = lambda k: pl.BlockSpec((tm, B_WIDTH), lambda i: (i, base + k))
    halo = lambda k: pl.BlockSpec((HALO, B_WIDTH), lambda i: (jnp.maximum(i * (tm // HALO) - 1, 0), base + k))
    per_layer = lambda shape: pl.BlockSpec((None,) + shape, lambda i: (layer,) + (0,) * len(shape))
    return pl.pallas_call(
        functools.partial(_mix_kernel, is_last=is_last, tiles_per_seq=seq // tm),
        out_shape=jax.ShapeDtypeStruct((n_rows, D_MODEL), F32),
        grid=(n_rows // tm,),
        in_specs=[
            pl.BlockSpec((tm, A_WIDTH), lambda i: (i, 0)),
            pblk(0), pblk(1), pblk(2),
            pblk(3), pblk(4), pblk(5), pblk(6),
            halo(3), halo(5),
            per_layer((1, B_WIDTH)), per_layer((1, B_WIDTH)),
            per_layer((B_GROUPS, B_CHUNK, B_CHUNK)),
            per_layer((B_GROUPS, B_CHUNK, LANES)),
            per_layer((CONV_K, C_WIDTH)),
            per_layer((D_MODEL, D_MODEL)),
            pl.BlockSpec((tm, D_MODEL), lambda i: (i, 0)),
            pl.BlockSpec((1, D_MODEL), lambda i: (0, 0)),
        ],
        out_specs=pl.BlockSpec((tm, D_MODEL), lambda i: (i, 0)),
        scratch_shapes=[
            pltpu.VMEM((tm, B_WIDTH + C_WIDTH), BF16),
            pltpu.VMEM((HALO + tm, C_WIDTH), F32),
        ],
        compiler_params=pltpu.CompilerParams(
            dimension_semantics=("arbitrary",), vmem_limit_bytes=VMEM_LIMIT),
        name="mix_out",
    )(ya2, proj2, proj2, proj2, proj2, proj2, proj2, proj2, proj2, proj2,
      ln_g, ln_b, w_s, bs_b, conv_w, wout_bf16, x2, fnw)


def _rotary_tables(seq):
    half = QK_DIM // 2
    pos = jnp.arange(seq, dtype=F32)
    inv_freq = ROPE_THETA ** (-jnp.arange(0, QK_DIM, 2, dtype=F32) / QK_DIM)
    ang = pos[:, None] * inv_freq[None, :]
    cos = jnp.tile(jnp.cos(ang), (1, LANES // half))
    sin = jnp.tile(jnp.sin(ang), (1, LANES // half))
    sign = jnp.where(jnp.arange(LANES) % QK_DIM < half, -1.0, 1.0).astype(F32)
    return jnp.stack([cos, sin * sign])


def kernel(x, norm_w, w_in, lam_q1, lam_k1, lam_q2, lam_k2, subln_w, sgu_ln_g, sgu_ln_b, w_s, b_s,
           conv_w, w_out, final_norm_w):
    b, seq, d = x.shape
    depth = norm_w.shape[0]
    tables = _rotary_tables(seq)
    x2 = x.reshape(b * seq, d)
    fnw = final_norm_w.reshape(1, d)
    norm_w3 = norm_w.reshape(depth, 1, d)
    lamv = jnp.stack([lam_q1, lam_k1, lam_q2, lam_k2], axis=1)
    subln_w3 = subln_w.reshape(depth, 1, HEAD_W)
    ln_g3 = sgu_ln_g.reshape(depth, 1, B_WIDTH)
    ln_b3 = sgu_ln_b.reshape(depth, 1, B_WIDTH)
    bs_b = jnp.broadcast_to(b_s[..., None], (depth, B_GROUPS, B_CHUNK, LANES))
    w_out_bf16 = w_out.astype(BF16)
    for l in range(depth):
        lam_init = 0.8 - 0.6 * math.exp(-0.3 * l)
        proj2 = _project(x2, norm_w3, w_in, l)
        ya = _attention(proj2.reshape(b, seq, PROJ_COLS), lamv, subln_w3, tables, l, lam_init)
        x2 = _mix_out(ya.reshape(b * seq, A_WIDTH), proj2, x2, ln_g3, ln_b3, w_s, bs_b, conv_w,
                      w_out_bf16, l, fnw, seq, l == depth - 1)
    return x2.reshape(b, seq, d)
```

```python
import functools
import math

import jax
import jax.numpy as jnp
from jax import lax
from jax.experimental import pallas as pl
from jax.experimental.pallas import tpu as pltpu

F32 = jnp.float32
BF16 = jnp.bfloat16

D_MODEL = 2048
A_HEADS = 8
QK_DIM = 64
HEAD_W = 2 * QK_DIM
A_WIDTH = A_HEADS * HEAD_W
B_GROUPS = 4
B_CHUNK = 128
B_WIDTH = 512
C_WIDTH = 512
CONV_K = 3
PROJ_COLS = 4 * A_WIDTH + 3 * B_WIDTH + 4 * C_WIDTH
ROPE_THETA = 10000.0
NORM_EPS = 1e-5
LN_EPS = 1e-5
LOG2E = 1.4426950408889634

LANES = 128
VMEM_LIMIT = 60 * 1024 * 1024

PROJ_TM = 4096
PROJ_XB = 1024
PROJ_NORM_STEPS = PROJ_TM // PROJ_XB
PROJ_TN = 512
NORM_ROWS = 128
ATT_TK = 512
ATT_TQ = 2 * ATT_TK
BF16_SUBLANES = 16
VT_ROWS = HEAD_W + BF16_SUBLANES
MIX_TM = 512
HALO = 8

NEG_BIG = -1e30


def _silu(g):
    return g * jax.nn.sigmoid(g)


def _proj_kernel(x_ref, nw_ref, w_ref, o_ref, h_scr):
    j = pl.program_id(1)

    @pl.when(j < PROJ_NORM_STEPS)
    def _():
        nw = nw_ref[...]

        def norm_rows(r, carry):
            x = x_ref[pl.ds(pl.multiple_of(r * NORM_ROWS, NORM_ROWS), NORM_ROWS), :]
            ms = jnp.mean(x * x, axis=-1, keepdims=True)
            dst = pl.ds(pl.multiple_of(j * PROJ_XB + r * NORM_ROWS, NORM_ROWS), NORM_ROWS)
            h_scr[dst, :] = (x * lax.rsqrt(ms + NORM_EPS) * nw).astype(BF16)
            return carry

        lax.fori_loop(0, PROJ_XB // NORM_ROWS, norm_rows, 0)

    @pl.when(j >= PROJ_NORM_STEPS)
    def _():
        o_ref[...] = jnp.dot(h_scr[...], w_ref[...].astype(BF16), preferred_element_type=F32).astype(BF16)


def _project(x2, norm_w, w_in, layer):
    n_rows = x2.shape[0]
    col_tile = lambda j: jnp.maximum(j - PROJ_NORM_STEPS, 0)
    return pl.pallas_call(
        _proj_kernel,
        out_shape=jax.ShapeDtypeStruct((n_rows, PROJ_COLS), BF16),
        grid=(n_rows // PROJ_TM, PROJ_NORM_STEPS + PROJ_COLS // PROJ_TN),
        in_specs=[
            pl.BlockSpec((PROJ_XB, D_MODEL),
                         lambda i, j: (i * PROJ_NORM_STEPS + jnp.minimum(j, PROJ_NORM_STEPS - 1), 0)),
            pl.BlockSpec((None, 1, D_MODEL), lambda i, j: (layer, 0, 0)),
            pl.BlockSpec((None, D_MODEL, PROJ_TN), lambda i, j: (layer, 0, col_tile(j))),
        ],
        out_specs=pl.BlockSpec((PROJ_TM, PROJ_TN), lambda i, j: (i, col_tile(j))),
        scratch_shapes=[pltpu.VMEM((PROJ_TM, D_MODEL), BF16)],
        compiler_params=pltpu.CompilerParams(
            dimension_semantics=("arbitrary", "arbitrary"), vmem_limit_bytes=VMEM_LIMIT),
        name="proj",
    )(x2, norm_w, w_in)


def _rotary(x, swap, cos, sin):
    partner = jnp.dot(x, swap, preferred_element_type=F32)
    return x.astype(F32) * cos + partner * sin


def _tile_rows(tile, size):
    if isinstance(tile, int):
        return pl.ds(tile * size, size)
    return pl.ds(pl.multiple_of(tile * size, size), size)


def _attn_kernel(lamv_ref, sw_ref, tab_ref, q_ref, k_ref, v_ref, g_ref, o_ref,
                 kr_scr, vt_scr, qm_scr, s_scr, acc_scr, bm_scr, mask_scr, swap_scr, *, lam_init, seq):
    tk, tq = ATT_TK, ATT_TQ
    n_q = seq // tq
    nt = (((1,), (1,)), ((), ()))
    lo, hi = slice(0, tk), slice(tk, tq)

    def prep_q(tile, dst):
        rows = _tile_rows(tile, tq)
        qf = (_rotary(q_ref[0, rows, :], swap_scr[...], tab_ref[0, rows, :], tab_ref[1, rows, :])
              * (QK_DIM ** -0.5 * LOG2E))
        lane = lax.broadcasted_iota(jnp.int32, (tq, HEAD_W), 1)
        is_c1 = lane < QK_DIM
        qm_scr[dst, 0] = jnp.where(is_c1, qf, 0.0).astype(BF16)
        qm_scr[dst, 1] = jnp.where(is_c1, 0.0, qf).astype(BF16)

    def produce(kt, buf, q_slot, c, cols, masked=False):
        s = lax.dot_general(kr_scr[_tile_rows(kt, tk), :], qm_scr[q_slot, c, cols, :], nt,
                            preferred_element_type=F32)
        if masked:
            s = s + mask_scr[...]
        s_scr[buf, c, :, cols] = s
        return jnp.max(s, axis=0, keepdims=True)

    def consume(kt, buf, c, cols, m_old, bmax, masked=False):
        s = s_scr[buf, c, :, cols]
        if masked:
            s = s + mask_scr[...]
            bmax = jnp.max(s, axis=0, keepdims=True)
        m_new = jnp.maximum(m_old, bmax)
        p = jnp.exp2(s - m_new).astype(BF16)
        acc_scr[c, :, cols] = (jnp.exp2(m_old - m_new) * acc_scr[c, :, cols]
                               + jnp.dot(vt_scr[kt], p, preferred_element_type=F32))
        return m_new

    @pl.when((pl.program_id(0) == 0) & (pl.program_id(1) == 0))
    def _():
        row = lax.broadcasted_iota(jnp.int32, (tk, tk), 0)
        col = lax.broadcasted_iota(jnp.int32, (tk, tk), 1)
        mask_scr[...] = jnp.where(row <= col, 0.0, NEG_BIG)
        src = lax.broadcasted_iota(jnp.int32, (HEAD_W, HEAD_W), 0)
        dst = lax.broadcasted_iota(jnp.int32, (HEAD_W, HEAD_W), 1)
        swap_scr[...] = jnp.where(src == (dst ^ (QK_DIM // 2)), 1.0, 0.0).astype(BF16)

    def prep_kv(t):
        rows = slice(t * tk, (t + 1) * tk)
        vt_scr[t, :HEAD_W, :] = v_ref[0, rows, :].T
        vt_scr[t, HEAD_W:, :] = jnp.ones((VT_ROWS - HEAD_W, tk), BF16)
        kr_scr[rows, :] = _rotary(k_ref[0, rows, :], swap_scr[...], tab_ref[0, rows, :],
                                  tab_ref[1, rows, :]).astype(BF16)

    prep_kv(0)
    prep_q(0, 0)
    blocks = [(c, cols) for c in range(2) for cols in (lo, hi)]
    for t in range(1, seq // tk):
        if blocks:
            c, cols = blocks.pop(0)
            bm_scr[c, :, cols] = produce(0, 0, 0, c, cols)
        prep_kv(t)
    for c, cols in blocks:
        bm_scr[c, :, cols] = produce(0, 0, 0, c, cols)

    def q_tile(qi, _):
        slot = qi & 1
        acc_scr[...] = jnp.zeros_like(acc_scr)

        def produce_consume(kt_p, buf_p, kt_c, buf_c, stats, bmax_c):
            new_stats, new_bmax = [], []
            for i, (c, cols) in enumerate((c, cols) for c in range(2) for cols in (lo, hi)):
                new_bmax.append(produce(kt_p, buf_p, slot, c, cols))
                new_stats.append(consume(kt_c, buf_c, c, cols, stats[i], bmax_c[i]))
            return tuple(new_stats), tuple(new_bmax)

        def pair(j, carry):
            stats, bm_a = carry[:4], carry[4:]
            stats, bm_b = produce_consume(2 * j + 1, 1, 2 * j, 0, stats, bm_a)
            stats, bm_a = produce_consume(2 * j + 2, 0, 2 * j + 1, 1, stats, bm_b)
            return stats + bm_a

        n_full = 2 * qi
        stats0 = (jnp.full((1, tk), NEG_BIG, F32),) * 4
        bm0 = tuple(bm_scr[c, :, cols] for c in range(2) for cols in (lo, hi))
        carry = lax.fori_loop(0, qi // 2, lambda j, cr: pair(2 * j + 1, pair(2 * j, cr)), stats0 + bm0)
        carry = lax.cond(qi % 2 == 1, lambda cr: pair(qi - 1, cr), lambda cr: cr, carry)

        def tail(with_next):
            (m1_lo, m1_hi, m2_lo, m2_hi), (_, b1_hi, _, b2_hi) = carry[:4], carry[4:]

            def produce_next(c, cols):
                bm_scr[c, :, cols] = produce(0, 0, 1 - slot, c, cols)

            def epilogue(cols):
                rows = pl.ds(pl.multiple_of(qi * tq + cols.start, tk), tk)
                lamv = lamv_ref[...]
                lam = (jnp.exp(jnp.sum(lamv[0:1] * lamv[1:2], axis=-1, keepdims=True))
                       - jnp.exp(jnp.sum(lamv[2:3] * lamv[3:4], axis=-1, keepdims=True)) + lam_init)
                l1 = acc_scr[0, HEAD_W:HEAD_W + 1, cols]
                l2 = acc_scr[1, HEAD_W:HEAD_W + 1, cols]
                d = acc_scr[0, :HEAD_W, cols] * (1.0 / l1) - lam * (acc_scr[1, :HEAD_W, cols] * (1.0 / l2))
                ms = jnp.mean(d * d, axis=0, keepdims=True)
                y = (d * lax.rsqrt(ms + NORM_EPS)).T
                g = g_ref[0, rows, :].astype(F32)
                o_ref[0, rows, :] = (y * (sw_ref[...] * (1.0 - lam_init)) * _silu(g)).astype(BF16)

            b1_sub = produce(n_full + 1, 1, slot, 0, hi, masked=True)
            if with_next:
                prep_q(qi + 1, 1 - slot)
            consume(n_full, 0, 0, lo, m1_lo, None, masked=True)
            b2_sub = produce(n_full + 1, 1, slot, 1, hi, masked=True)
            m1_hi = consume(n_full, 0, 0, hi, m1_hi, b1_hi)
            if with_next:
                produce_next(0, lo)
            consume(n_full, 0, 1, lo, m2_lo, None, masked=True)
            if with_next:
                produce_next(0, hi)
            m2_hi = consume(n_full, 0, 1, hi, m2_hi, b2_hi)
            if with_next:
                produce_next(1, lo)
            epilogue(lo)
            consume(n_full + 1, 1, 0, hi, m1_hi, b1_sub)
            consume(n_full + 1, 1, 1, hi, m2_hi, b2_sub)
            if with_next:
                produce_next(1, hi)
            epilogue(hi)

        pl.when(qi < n_q - 1)(lambda: tail(True))
        pl.when(qi == n_q - 1)(lambda: tail(False))
        return 0

    lax.fori_loop(0, n_q, q_tile, 0)


def _attention(proj3, lamv, subln_w, tables, layer, lam_init):
    b, seq, _ = proj3.shape
    tk, tq = ATT_TK, ATT_TQ
    head_blk = lambda off: pl.BlockSpec((1, seq, HEAD_W), lambda bi, h: (bi, 0, off + h))
    return pl.pallas_call(
        functools.partial(_attn_kernel, lam_init=lam_init, seq=seq),
        out_shape=jax.ShapeDtypeStruct((b, seq, A_WIDTH), BF16),
        grid=(b, A_HEADS),
        in_specs=[
            pl.BlockSpec((None, 4, QK_DIM), lambda bi, h: (layer, 0, 0)),
            pl.BlockSpec((None, 1, HEAD_W), lambda bi, h: (layer, 0, 0)),
            pl.BlockSpec((2, seq, LANES), lambda bi, h: (0, 0, 0)),
            head_blk(0), head_blk(A_HEADS), head_blk(2 * A_HEADS), head_blk(3 * A_HEADS),
        ],
        out_specs=pl.BlockSpec((1, seq, HEAD_W), lambda bi, h: (bi, 0, h)),
        scratch_shapes=[
            pltpu.VMEM((seq, HEAD_W), BF16),
            pltpu.VMEM((seq // tk, VT_ROWS, tk), BF16),
            pltpu.VMEM((2, 2, tq, HEAD_W), BF16),
            pltpu.VMEM((2, 2, tk, tq), F32),
            pltpu.VMEM((2, VT_ROWS, tq), F32),
            pltpu.VMEM((2, 1, tq), F32),
            pltpu.VMEM((tk, tk), F32),
            pltpu.VMEM((HEAD_W, HEAD_W), BF16),
        ],
        compiler_params=pltpu.CompilerParams(
            dimension_semantics=("arbitrary", "arbitrary"), vmem_limit_bytes=VMEM_LIMIT),
        name="diff_attn",
    )(lamv, subln_w, tables, proj3, proj3, proj3, proj3)


def _mix_kernel(ya_ref, u_ref, vs_ref, gb_ref, xc_ref, bg_ref, cg_ref, gc_ref, xcp_ref, cgp_ref,
                lng_ref, lnb_ref, ws_ref, bs_ref, cw_ref, wout_ref, x_ref, fnw_ref, o_ref,
                cat_scr, z_scr, *, is_last, tiles_per_seq):
    tm = MIX_TM
    n_chunks = tm // B_CHUNK
    i = pl.program_id(0)

    seq_start = (i % tiles_per_seq) == 0
    zp = cgp_ref[...].astype(F32) * xcp_ref[...].astype(F32)
    z_scr[0:HALO, :] = jnp.where(seq_start, 0.0, zp)
    z_scr[HALO:HALO + tm, :] = cg_ref[...].astype(F32) * xc_ref[...].astype(F32)

    tri_row = lax.broadcasted_iota(jnp.int32, (B_CHUNK, B_CHUNK), 0)
    tri_col = lax.broadcasted_iota(jnp.int32, (B_CHUNK, B_CHUNK), 1)
    w_tril = [jnp.where(tri_row >= tri_col, ws_ref[g], 0.0).astype(BF16) for g in range(B_GROUPS)]
    cw = cw_ref[...]

    def mix_b(c):
        rows = slice(c * B_CHUNK, (c + 1) * B_CHUNK)
        v = vs_ref[rows, :].astype(F32)
        mu = jnp.mean(v, axis=-1, keepdims=True)
        vc = v - mu
        var = jnp.mean(vc * vc, axis=-1, keepdims=True)
        vn = (vc * lax.rsqrt(var + LN_EPS) * lng_ref[...] + lnb_ref[...]).astype(BF16)
        for g in range(B_GROUPS):
            cols = slice(g * LANES, (g + 1) * LANES)
            mixed = jnp.dot(w_tril[g], vn[:, cols], preferred_element_type=F32) + bs_ref[g]
            yb = u_ref[rows, cols].astype(F32) * mixed * _silu(gb_ref[rows, cols].astype(F32))
            cat_scr[rows, cols] = yb.astype(BF16)

    def mix_c(c):
        rows = slice(c * B_CHUNK, (c + 1) * B_CHUNK)
        conv = cw[CONV_K - 1:CONV_K] * z_scr[HALO + c * B_CHUNK:HALO + (c + 1) * B_CHUNK, :]
        for tap in range(CONV_K - 1):
            start = HALO + c * B_CHUNK - (CONV_K - 1 - tap)
            conv = conv + cw[tap:tap + 1] * z_scr[start:start + B_CHUNK, :]
        yc = bg_ref[rows, :].astype(F32) * conv * _silu(gc_ref[rows, :].astype(F32))
        cat_scr[rows, B_WIDTH:B_WIDTH + C_WIDTH] = yc.astype(BF16)

    cw_out = D_MODEL // n_chunks
    w_b = slice(A_WIDTH, A_WIDTH + B_WIDTH)
    w_c = slice(A_WIDTH + B_WIDTH, D_MODEL)
    for c in range(n_chunks):
        cols = slice(c * cw_out, (c + 1) * cw_out)
        o_ref[:, cols] = x_ref[:, cols] + jnp.dot(ya_ref[...], wout_ref[0:A_WIDTH, cols],
                                                  preferred_element_type=F32)
        mix_b(c)
    for c in range(n_chunks):
        cols = slice(c * cw_out, (c + 1) * cw_out)
        o_ref[:, cols] += jnp.dot(cat_scr[:, 0:B_WIDTH], wout_ref[w_b, cols], preferred_element_type=F32)
        mix_c(c)
    out = o_ref[...] + jnp.dot(cat_scr[:, B_WIDTH:], wout_ref[w_c, :], preferred_element_type=F32)
    if is_last:
        ms = jnp.mean(out * out, axis=-1, keepdims=True)
        out = out * lax.rsqrt(ms + NORM_EPS) * fnw_ref[...]
    o_ref[...] = out


def _mix_out(ya2, proj2, x2, ln_g, ln_b, w_s, bs_b, conv_w, wout_bf16, layer, fnw, seq, is_last):
    n_rows = x2.shape[0]
    tm = MIX_TM
    base = (4 * A_WIDTH) // B_WIDTH
    pblk = lambda k: pl.BlockSpec((tm, B_WIDTH), lambda i: (i, base + k))
    halo = lambda k: pl.BlockSpec((HALO, B_WIDTH), lambda i: (jnp.maximum(i * (tm // HALO) - 1, 0), base + k))
    per_layer = lambda shape: pl.BlockSpec((None,) + shape, lambda i: (layer,) + (0,) * len(shape))
    return pl.pallas_call(
        functools.partial(_mix_kernel, is_last=is_last, tiles_per_seq=seq // tm),
        out_shape=jax.ShapeDtypeStruct((n_rows, D_MODEL), F32),
        grid=(n_rows // tm,),
        in_specs=[
            pl.BlockSpec((tm, A_WIDTH), lambda i: (i, 0)),
            pblk(0), pblk(1), pblk(2),
            pblk(3), pblk(4), pblk(5), pblk(6),
            halo(3), halo(5),
            per_layer((1, B_WIDTH)), per_layer((1, B_WIDTH)),
            per_layer((B_GROUPS, B_CHUNK, B_CHUNK)),
            per_layer((B_GROUPS, B_CHUNK, LANES)),
            per_layer((CONV_K, C_WIDTH)),
            per_layer((D_MODEL, D_MODEL)),
            pl.BlockSpec((tm, D_MODEL), lambda i: (i, 0)),
            pl.BlockSpec((1, D_MODEL), lambda i: (0, 0)),
        ],
        out_specs=pl.BlockSpec((tm, D_MODEL), lambda i: (i, 0)),
        scratch_shapes=[
            pltpu.VMEM((tm, B_WIDTH + C_WIDTH), BF16),
            pltpu.VMEM((HALO + tm, C_WIDTH), F32),
        ],
        compiler_params=pltpu.CompilerParams(
            dimension_semantics=("arbitrary",), vmem_limit_bytes=VMEM_LIMIT),
        name="mix_out",
    )(ya2, proj2, proj2, proj2, proj2, proj2, proj2, proj2, proj2, proj2,
      ln_g, ln_b, w_s, bs_b, conv_w, wout_bf16, x2, fnw)


def _rotary_tables(seq):
    half = QK_DIM // 2
    pos = jnp.arange(seq, dtype=F32)
    inv_freq = ROPE_THETA ** (-jnp.arange(0, QK_DIM, 2, dtype=F32) / QK_DIM)
    ang = pos[:, None] * inv_freq[None, :]
    cos = jnp.tile(jnp.cos(ang), (1, LANES // half))
    sin = jnp.tile(jnp.sin(ang), (1, LANES // half))
    sign = jnp.where(jnp.arange(LANES) % QK_DIM < half, -1.0, 1.0).astype(F32)
    return jnp.stack([cos, sin * sign])


def kernel(x, norm_w, w_in, lam_q1, lam_k1, lam_q2, lam_k2, subln_w, sgu_ln_g, sgu_ln_b, w_s, b_s,
           conv_w, w_out, final_norm_w):
    b, seq, d = x.shape
    depth = norm_w.shape[0]
    tables = _rotary_tables(seq)
    x2 = x.reshape(b * seq, d)
    fnw = final_norm_w.reshape(1, d)
    norm_w3 = norm_w.reshape(depth, 1, d)
    lamv = jnp.stack([lam_q1, lam_k1, lam_q2, lam_k2], axis=1)
    subln_w3 = subln_w.reshape(depth, 1, HEAD_W)
    ln_g3 = sgu_ln_g.reshape(depth, 1, B_WIDTH)
    ln_b3 = sgu_ln_b.reshape(depth, 1, B_WIDTH)
    bs_b = jnp.broadcast_to(b_s[..., None], (depth, B_GROUPS, B_CHUNK, LANES))
    w_out_bf16 = w_out.astype(BF16)
    for l in range(depth):
        lam_init = 0.8 - 0.6 * math.exp(-0.3 * l)
        proj2 = _project(x2, norm_w3, w_in, l)
        ya = _attention(proj2.reshape(b, seq, PROJ_COLS), lamv, subln_w3, tables, l, lam_init)
        x2 = _mix_out(ya.reshape(b * seq, A_WIDTH), proj2, x2, ln_g3, ln_b3, w_s, bs_b, conv_w,
                      w_out_bf16, l, fnw, seq, l == depth - 1)
    return x2.reshape(b, seq, d)
```

```python
import functools
import math

import jax
import jax.numpy as jnp
from jax import lax
from jax.experimental import pallas as pl
from jax.experimental.pallas import tpu as pltpu

F32 = jnp.float32
BF16 = jnp.bfloat16

D_MODEL = 2048
A_HEADS = 8
QK_DIM = 64
HEAD_W = 2 * QK_DIM
A_WIDTH = A_HEADS * HEAD_W
B_GROUPS = 4
B_CHUNK = 128
B_WIDTH = 512
C_WIDTH = 512
CONV_K = 3
PROJ_COLS = 4 * A_WIDTH + 3 * B_WIDTH + 4 * C_WIDTH
ROPE_THETA = 10000.0
NORM_EPS = 1e-5
LN_EPS = 1e-5
LOG2E = 1.4426950408889634

LANES = 128
MIB = 1024 * 1024
V7X_VMEM = 64 * MIB
PROJ_VMEM = 60 * MIB
ATTN_VMEM = 40 * MIB
MIX_VMEM = 48 * MIB
assert max(PROJ_VMEM, ATTN_VMEM, MIX_VMEM) < V7X_VMEM

PROJ_TM = 4096
PROJ_XB = 1024
PROJ_NORM_STEPS = PROJ_TM // PROJ_XB
PROJ_TN = 512
NORM_ROWS = 128
ATT_TK = 512
ATT_TQ = 2 * ATT_TK
BF16_SUBLANES = 16
VT_ROWS = HEAD_W + BF16_SUBLANES
MIX_TM = 512
HALO = 8

NEG_BIG = -1e30


def _silu(g):
    return g * jax.nn.sigmoid(g)


def _proj_kernel(x_ref, nw_ref, w_ref, o_ref, h_scr):
    j = pl.program_id(1)

    @pl.when(j < PROJ_NORM_STEPS)
    def _():
        nw = nw_ref[...]

        def norm_rows(r, carry):
            x = x_ref[pl.ds(pl.multiple_of(r * NORM_ROWS, NORM_ROWS), NORM_ROWS), :]
            ms = jnp.mean(x * x, axis=-1, keepdims=True)
            dst = pl.ds(pl.multiple_of(j * PROJ_XB + r * NORM_ROWS, NORM_ROWS), NORM_ROWS)
            h_scr[dst, :] = (x * lax.rsqrt(ms + NORM_EPS) * nw).astype(BF16)
            return carry

        lax.fori_loop(0, PROJ_XB // NORM_ROWS, norm_rows, 0)

    @pl.when(j >= PROJ_NORM_STEPS)
    def _():
        o_ref[...] = jnp.dot(h_scr[...], w_ref[...].astype(BF16), preferred_element_type=F32).astype(BF16)


def _project(x2, norm_w, w_in, layer):
    n_rows = x2.shape[0]
    col_tile = lambda j: jnp.maximum(j - PROJ_NORM_STEPS, 0)
    return pl.pallas_call(
        _proj_kernel,
        out_shape=jax.ShapeDtypeStruct((n_rows, PROJ_COLS), BF16),
        grid=(n_rows // PROJ_TM, PROJ_NORM_STEPS + PROJ_COLS // PROJ_TN),
        in_specs=[
            pl.BlockSpec((PROJ_XB, D_MODEL),
                         lambda i, j: (i * PROJ_NORM_STEPS + jnp.minimum(j, PROJ_NORM_STEPS - 1), 0)),
            pl.BlockSpec((None, 1, D_MODEL), lambda i, j: (layer, 0, 0)),
            pl.BlockSpec((None, D_MODEL, PROJ_TN), lambda i, j: (layer, 0, col_tile(j))),
        ],
        out_specs=pl.BlockSpec((PROJ_TM, PROJ_TN), lambda i, j: (i, col_tile(j))),
        scratch_shapes=[pltpu.VMEM((PROJ_TM, D_MODEL), BF16)],
        compiler_params=pltpu.CompilerParams(
            dimension_semantics=("arbitrary", "arbitrary"), vmem_limit_bytes=PROJ_VMEM),
        name="proj",
    )(x2, norm_w, w_in)


def _rotary(x, swap, cos, sin):
    partner = jnp.dot(x, swap, preferred_element_type=F32)
    return x.astype(F32) * cos + partner * sin


def _tile_rows(tile, size):
    if isinstance(tile, int):
        return pl.ds(tile * size, size)
    return pl.ds(pl.multiple_of(tile * size, size), size)


def _attn_kernel(lamv_ref, sw_ref, tab_ref, q_ref, k_ref, v_ref, g_ref, o_ref,
                 kr_scr, vt_scr, qm_scr, s_scr, acc_scr, bm_scr, mask_scr, swap_scr, *, lam_init, seq):
    tk, tq = ATT_TK, ATT_TQ
    n_q = seq // tq
    nt = (((1,), (1,)), ((), ()))
    lo, hi = slice(0, tk), slice(tk, tq)

    def prep_q(tile, dst):
        rows = _tile_rows(tile, tq)
        qf = (_rotary(q_ref[0, rows, :], swap_scr[...], tab_ref[0, rows, :], tab_ref[1, rows, :])
              * (QK_DIM ** -0.5 * LOG2E))
        lane = lax.broadcasted_iota(jnp.int32, (tq, HEAD_W), 1)
        is_c1 = lane < QK_DIM
        qm_scr[dst, 0] = jnp.where(is_c1, qf, 0.0).astype(BF16)
        qm_scr[dst, 1] = jnp.where(is_c1, 0.0, qf).astype(BF16)

    def produce(kt, buf, q_slot, c, cols, masked=False):
        s = lax.dot_general(kr_scr[_tile_rows(kt, tk), :], qm_scr[q_slot, c, cols, :], nt,
                            preferred_element_type=F32)
        if masked:
            s = s + mask_scr[...]
        s_scr[buf, c, :, cols] = s
        return jnp.max(s, axis=0, keepdims=True)

    def consume(kt, buf, c, cols, m_old, bmax, masked=False):
        s = s_scr[buf, c, :, cols]
        if masked:
            s = s + mask_scr[...]
            bmax = jnp.max(s, axis=0, keepdims=True)
        m_new = jnp.maximum(m_old, bmax)
        p = jnp.exp2(s - m_new).astype(BF16)
        acc_scr[c, :, cols] = (jnp.exp2(m_old - m_new) * acc_scr[c, :, cols]
                               + jnp.dot(vt_scr[kt], p, preferred_element_type=F32))
        return m_new

    @pl.when((pl.program_id(0) == 0) & (pl.program_id(1) == 0))
    def _():
        row = lax.broadcasted_iota(jnp.int32, (tk, tk), 0)
        col = lax.broadcasted_iota(jnp.int32, (tk, tk), 1)
        mask_scr[...] = jnp.where(row <= col, 0.0, NEG_BIG)
        src = lax.broadcasted_iota(jnp.int32, (HEAD_W, HEAD_W), 0)
        dst = lax.broadcasted_iota(jnp.int32, (HEAD_W, HEAD_W), 1)
        swap_scr[...] = jnp.where(src == (dst ^ (QK_DIM // 2)), 1.0, 0.0).astype(BF16)

    def prep_kv(t):
        rows = slice(t * tk, (t + 1) * tk)
        vt_scr[t, :HEAD_W, :] = v_ref[0, rows, :].T
        vt_scr[t, HEAD_W:, :] = jnp.ones((VT_ROWS - HEAD_W, tk), BF16)
        kr_scr[rows, :] = _rotary(k_ref[0, rows, :], swap_scr[...], tab_ref[0, rows, :],
                                  tab_ref[1, rows, :]).astype(BF16)

    prep_kv(0)
    prep_q(0, 0)
    blocks = [(c, cols) for c in range(2) for cols in (lo, hi)]
    for t in range(1, seq // tk):
        if blocks:
            c, cols = blocks.pop(0)
            bm_scr[c, :, cols] = produce(0, 0, 0, c, cols)
        prep_kv(t)
    for c, cols in blocks:
        bm_scr[c, :, cols] = produce(0, 0, 0, c, cols)

    def q_tile(qi, _):
        slot = qi & 1
        acc_scr[...] = jnp.zeros_like(acc_scr)

        def produce_consume(kt_p, buf_p, kt_c, buf_c, stats, bmax_c):
            new_stats, new_bmax = [], []
            for i, (c, cols) in enumerate((c, cols) for c in range(2) for cols in (lo, hi)):
                new_bmax.append(produce(kt_p, buf_p, slot, c, cols))
                new_stats.append(consume(kt_c, buf_c, c, cols, stats[i], bmax_c[i]))
            return tuple(new_stats), tuple(new_bmax)

        def pair(j, carry):
            stats, bm_a = carry[:4], carry[4:]
            stats, bm_b = produce_consume(2 * j + 1, 1, 2 * j, 0, stats, bm_a)
            stats, bm_a = produce_consume(2 * j + 2, 0, 2 * j + 1, 1, stats, bm_b)
            return stats + bm_a

        n_full = 2 * qi
        stats0 = (jnp.full((1, tk), NEG_BIG, F32),) * 4
        bm0 = tuple(bm_scr[c, :, cols] for c in range(2) for cols in (lo, hi))
        carry = lax.fori_loop(0, qi // 2, lambda j, cr: pair(2 * j + 1, pair(2 * j, cr)), stats0 + bm0)
        carry = lax.cond(qi % 2 == 1, lambda cr: pair(qi - 1, cr), lambda cr: cr, carry)

        def tail(with_next):
            (m1_lo, m1_hi, m2_lo, m2_hi), (_, b1_hi, _, b2_hi) = carry[:4], carry[4:]

            def produce_next(c, cols):
                bm_scr[c, :, cols] = produce(0, 0, 1 - slot, c, cols)

            def epilogue(cols):
                rows = pl.ds(pl.multiple_of(qi * tq + cols.start, tk), tk)
                lamv = lamv_ref[...]
                lam = (jnp.exp(jnp.sum(lamv[0:1] * lamv[1:2], axis=-1, keepdims=True))
                       - jnp.exp(jnp.sum(lamv[2:3] * lamv[3:4], axis=-1, keepdims=True)) + lam_init)
                l1 = acc_scr[0, HEAD_W:HEAD_W + 1, cols]
                l2 = acc_scr[1, HEAD_W:HEAD_W + 1, cols]
                d = acc_scr[0, :HEAD_W, cols] * (1.0 / l1) - lam * (acc_scr[1, :HEAD_W, cols] * (1.0 / l2))
                ms = jnp.mean(d * d, axis=0, keepdims=True)
                y = (d * lax.rsqrt(ms + NORM_EPS)).T
                g = g_ref[0, rows, :].astype(F32)
                o_ref[0, rows, :] = (y * (sw_ref[...] * (1.0 - lam_init)) * _silu(g)).astype(BF16)

            b1_sub = produce(n_full + 1, 1, slot, 0, hi, masked=True)
            if with_next:
                prep_q(qi + 1, 1 - slot)
            consume(n_full, 0, 0, lo, m1_lo, None, masked=True)
            b2_sub = produce(n_full + 1, 1, slot, 1, hi, masked=True)
            m1_hi = consume(n_full, 0, 0, hi, m1_hi, b1_hi)
            if with_next:
                produce_next(0, lo)
            consume(n_full, 0, 1, lo, m2_lo, None, masked=True)
            if with_next:
                produce_next(0, hi)
            m2_hi = consume(n_full, 0, 1, hi, m2_hi, b2_hi)
            if with_next:
                produce_next(1, lo)
            epilogue(lo)
            consume(n_full + 1, 1, 0, hi, m1_hi, b1_sub)
            consume(n_full + 1, 1, 1, hi, m2_hi, b2_sub)
            if with_next:
                produce_next(1, hi)
            epilogue(hi)

        pl.when(qi < n_q - 1)(lambda: tail(True))
        pl.when(qi == n_q - 1)(lambda: tail(False))
        return 0

    lax.fori_loop(0, n_q, q_tile, 0)


def _attention(proj3, lamv, subln_w, tables, layer, lam_init):
    b, seq, _ = proj3.shape
    tk, tq = ATT_TK, ATT_TQ
    head_blk = lambda off: pl.BlockSpec((1, seq, HEAD_W), lambda bi, h: (bi, 0, off + h))
    return pl.pallas_call(
        functools.partial(_attn_kernel, lam_init=lam_init, seq=seq),
        out_shape=jax.ShapeDtypeStruct((b, seq, A_WIDTH), BF16),
        grid=(b, A_HEADS),
        in_specs=[
            pl.BlockSpec((None, 4, QK_DIM), lambda bi, h: (layer, 0, 0)),
            pl.BlockSpec((None, 1, HEAD_W), lambda bi, h: (layer, 0, 0)),
            pl.BlockSpec((2, seq, LANES), lambda bi, h: (0, 0, 0)),
            head_blk(0), head_blk(A_HEADS), head_blk(2 * A_HEADS), head_blk(3 * A_HEADS),
        ],
        out_specs=pl.BlockSpec((1, seq, HEAD_W), lambda bi, h: (bi, 0, h)),
        scratch_shapes=[
            pltpu.VMEM((seq, HEAD_W), BF16),
            pltpu.VMEM((seq // tk, VT_ROWS, tk), BF16),
            pltpu.VMEM((2, 2, tq, HEAD_W), BF16),
            pltpu.VMEM((2, 2, tk, tq), F32),
            pltpu.VMEM((2, VT_ROWS, tq), F32),
            pltpu.VMEM((2, 1, tq), F32),
            pltpu.VMEM((tk, tk), F32),
            pltpu.VMEM((HEAD_W, HEAD_W), BF16),
        ],
        compiler_params=pltpu.CompilerParams(
            dimension_semantics=("arbitrary", "arbitrary"), vmem_limit_bytes=ATTN_VMEM),
        name="diff_attn",
    )(lamv, subln_w, tables, proj3, proj3, proj3, proj3)


def _mix_kernel(ya_ref, u_ref, vs_ref, gb_ref, xc_ref, bg_ref, cg_ref, gc_ref, xcp_ref, cgp_ref,
                lng_ref, lnb_ref, ws_ref, bs_ref, cw_ref, wout_ref, x_ref, fnw_ref, o_ref,
                cat_scr, z_scr, *, is_last, tiles_per_seq):
    tm = MIX_TM
    n_chunks = tm // B_CHUNK
    i = pl.program_id(0)

    seq_start = (i % tiles_per_seq) == 0
    zp = cgp_ref[...].astype(F32) * xcp_ref[...].astype(F32)
    z_scr[0:HALO, :] = jnp.where(seq_start, 0.0, zp)
    z_scr[HALO:HALO + tm, :] = cg_ref[...].astype(F32) * xc_ref[...].astype(F32)

    tri_row = lax.broadcasted_iota(jnp.int32, (B_CHUNK, B_CHUNK), 0)
    tri_col = lax.broadcasted_iota(jnp.int32, (B_CHUNK, B_CHUNK), 1)
    w_tril = [jnp.where(tri_row >= tri_col, ws_ref[g], 0.0).astype(BF16) for g in range(B_GROUPS)]
    cw = cw_ref[...]

    def mix_b(c):
        rows = slice(c * B_CHUNK, (c + 1) * B_CHUNK)
        v = vs_ref[rows, :].astype(F32)
        mu = jnp.mean(v, axis=-1, keepdims=True)
        vc = v - mu
        var = jnp.mean(vc * vc, axis=-1, keepdims=True)
        vn = (vc * lax.rsqrt(var + LN_EPS) * lng_ref[...] + lnb_ref[...]).astype(BF16)
        for g in range(B_GROUPS):
            cols = slice(g * LANES, (g + 1) * LANES)
            mixed = jnp.dot(w_tril[g], vn[:, cols], preferred_element_type=F32) + bs_ref[g]
            yb = u_ref[rows, cols].astype(F32) * mixed * _silu(gb_ref[rows, cols].astype(F32))
            cat_scr[rows, cols] = yb.astype(BF16)

    def mix_c(c):
        rows = slice(c * B_CHUNK, (c + 1) * B_CHUNK)
        conv = cw[CONV_K - 1:CONV_K] * z_scr[HALO + c * B_CHUNK:HALO + (c + 1) * B_CHUNK, :]
        for tap in range(CONV_K - 1):
            start = HALO + c * B_CHUNK - (CONV_K - 1 - tap)
            conv = conv + cw[tap:tap + 1] * z_scr[start:start + B_CHUNK, :]
        yc = bg_ref[rows, :].astype(F32) * conv * _silu(gc_ref[rows, :].astype(F32))
        cat_scr[rows, B_WIDTH:B_WIDTH + C_WIDTH] = yc.astype(BF16)

    cw_out = D_MODEL // n_chunks
    w_b = slice(A_WIDTH, A_WIDTH + B_WIDTH)
    w_c = slice(A_WIDTH + B_WIDTH, D_MODEL)
    for c in range(n_chunks):
        cols = slice(c * cw_out, (c + 1) * cw_out)
        o_ref[:, cols] = x_ref[:, cols] + jnp.dot(ya_ref[...], wout_ref[0:A_WIDTH, cols],
                                                  preferred_element_type=F32)
        mix_b(c)
    for c in range(n_chunks):
        cols = slice(c * cw_out, (c + 1) * cw_out)
        o_ref[:, cols] += jnp.dot(cat_scr[:, 0:B_WIDTH], wout_ref[w_b, cols], preferred_element_type=F32)
        mix_c(c)
    out = o_ref[...] + jnp.dot(cat_scr[:, B_WIDTH:], wout_ref[w_c, :], preferred_element_type=F32)
    if is_last:
        ms = jnp.mean(out * out, axis=-1, keepdims=True)
        out = out * lax.rsqrt(ms + NORM_EPS) * fnw_ref[...]
    o_ref[...] = out


def _mix_out(ya2, proj2, x2, ln_g, ln_b, w_s, bs_b, conv_w, wout_bf16, layer, fnw, seq, is_last):
    n_rows = x2.shape[0]
    tm = MIX_TM
    base = (4 * A_WIDTH) // B_WIDTH
    pblk = lambda k: pl.BlockSpec((tm, B_WIDTH), lambda i: (i, base + k))
    halo = lambda k: pl.BlockSpec((HALO, B_WIDTH), lambda i: (jnp.maximum(i * (tm // HALO) - 1, 0), base + k))
    per_layer = lambda shape: pl.BlockSpec((None,) + shape, lambda i: (layer,) + (0,) * len(shape))
    return pl.pallas_call(
        functools.partial(_mix_kernel, is_last=is_last, tiles_per_seq=seq // tm),
        out_shape=jax.ShapeDtypeStruct((n_rows, D_MODEL), F32),
        grid=(n_rows // tm,),
        in_specs=[
            pl.BlockSpec((tm, A_WIDTH), lambda i: (i, 0)),
            pblk(0), pblk(1), pblk(2),
            pblk(3), pblk(4), pblk(5), pblk(6),
            halo(3), halo(5),
            per_layer((1, B_WIDTH)), per_layer((1, B_WIDTH)),
            per_layer((B_GROUPS, B_CHUNK, B_CHUNK)),
            per_layer((B_GROUPS, B_CHUNK, LANES)),
            per_layer((CONV_K, C_WIDTH)),
            per_layer((D_MODEL, D_MODEL)),
            pl.BlockSpec((tm, D_MODEL), lambda i: (i, 0)),
            pl.BlockSpec((1, D_MODEL), lambda i: (0, 0)),
        ],
        out_specs=pl.BlockSpec((tm, D_MODEL), lambda i: (i, 0)),
        scratch_shapes=[
            pltpu.VMEM((tm, B_WIDTH + C_WIDTH), BF16),
            pltpu.VMEM((HALO + tm, C_WIDTH), F32),
        ],
        compiler_params=pltpu.CompilerParams(
            dimension_semantics=("arbitrary",), vmem_limit_bytes=MIX_VMEM),
        name="mix_out",
    )(ya2, proj2, proj2, proj2, proj2, proj2, proj2, proj2, proj2, proj2,
      ln_g, ln_b, w_s, bs_b, conv_w, wout_bf16, x2, fnw)


def _rotary_tables(seq):
    half = QK_DIM // 2
    pos = jnp.arange(seq, dtype=F32)
    inv_freq = ROPE_THETA ** (-jnp.arange(0, QK_DIM, 2, dtype=F32) / QK_DIM)
    ang = pos[:, None] * inv_freq[None, :]
    cos = jnp.tile(jnp.cos(ang), (1, LANES // half))
    sin = jnp.tile(jnp.sin(ang), (1, LANES // half))
    sign = jnp.where(jnp.arange(LANES) % QK_DIM < half, -1.0, 1.0).astype(F32)
    return jnp.stack([cos, sin * sign])


def kernel(x, norm_w, w_in, lam_q1, lam_k1, lam_q2, lam_k2, subln_w, sgu_ln_g, sgu_ln_b, w_s, b_s,
           conv_w, w_out, final_norm_w):
    b, seq, d = x.shape
    depth = norm_w.shape[0]
    tables = _rotary_tables(seq)
    x2 = x.reshape(b * seq, d)
    fnw = final_norm_w.reshape(1, d)
    norm_w3 = norm_w.reshape(depth, 1, d)
    lamv = jnp.stack([lam_q1, lam_k1, lam_q2, lam_k2], axis=1)
    subln_w3 = subln_w.reshape(depth, 1, HEAD_W)
    ln_g3 = sgu_ln_g.reshape(depth, 1, B_WIDTH)
    ln_b3 = sgu_ln_b.reshape(depth, 1, B_WIDTH)
    bs_b = jnp.broadcast_to(b_s[..., None], (depth, B_GROUPS, B_CHUNK, LANES))
    w_out_bf16 = w_out.astype(BF16)
    for l in range(depth):
        lam_init = 0.8 - 0.6 * math.exp(-0.3 * l)
        proj2 = _project(x2, norm_w3, w_in, l)
        ya = _attention(proj2.reshape(b, seq, PROJ_COLS), lamv, subln_w3, tables, l, lam_init)
        x2 = _mix_out(ya.reshape(b * seq, A_WIDTH), proj2, x2, ln_g3, ln_b3, w_s, bs_b, conv_w,
                      w_out_bf16, l, fnw, seq, l == depth - 1)
    return x2.reshape(b, seq, d)
```

```python
import functools
import math

import jax
import jax.numpy as jnp
from jax import lax
from jax.experimental import pallas as pl
from jax.experimental.pallas import tpu as pltpu

F32 = jnp.float32
BF16 = jnp.bfloat16

D_MODEL = 2048
A_HEADS = 8
QK_DIM = 64
HEAD_W = 2 * QK_DIM
A_WIDTH = A_HEADS * HEAD_W
B_GROUPS = 4
B_CHUNK = 128
B_WIDTH = 512
C_WIDTH = 512
CONV_K = 3
PROJ_COLS = 4 * A_WIDTH + 3 * B_WIDTH + 4 * C_WIDTH
ROPE_THETA = 10000.0
NORM_EPS = 1e-5
LN_EPS = 1e-5
LOG2E = 1.4426950408889634

LANES = 128
MIB = 1024 * 1024
V7X_VMEM = 64 * MIB
PROJ_VMEM = 60 * MIB
ATTN_VMEM = 40 * MIB
MIX_VMEM = 48 * MIB
assert max(PROJ_VMEM, ATTN_VMEM, MIX_VMEM) < V7X_VMEM

PROJ_TM = 4096
PROJ_XB = 1024
PROJ_NORM_STEPS = PROJ_TM // PROJ_XB
PROJ_TN = 512
NORM_ROWS = 128
ATT_TK = 512
ATT_TQ = 2 * ATT_TK
BF16_SUBLANES = 16
VT_ROWS = HEAD_W + BF16_SUBLANES
MIX_TM = 512
HALO = 8

NEG_BIG = -1e30


def _silu(g):
    return g * jax.nn.sigmoid(g)


def _proj_kernel(x_ref, nw_ref, w_ref, o_ref, h_scr):
    j = pl.program_id(1)

    @pl.when(j < PROJ_NORM_STEPS)
    def _():
        nw = nw_ref[...]

        def norm_rows(r, carry):
            x = x_ref[pl.ds(pl.multiple_of(r * NORM_ROWS, NORM_ROWS), NORM_ROWS), :]
            ms = jnp.mean(x * x, axis=-1, keepdims=True)
            dst = pl.ds(pl.multiple_of(j * PROJ_XB + r * NORM_ROWS, NORM_ROWS), NORM_ROWS)
            h_scr[dst, :] = (x * lax.rsqrt(ms + NORM_EPS) * nw).astype(BF16)
            return carry

        lax.fori_loop(0, PROJ_XB // NORM_ROWS, norm_rows, 0)

    @pl.when(j >= PROJ_NORM_STEPS)
    def _():
        o_ref[...] = jnp.dot(h_scr[...], w_ref[...].astype(BF16), preferred_element_type=F32).astype(BF16)


def _project(x2, norm_w, w_in, layer):
    n_rows = x2.shape[0]
    col_tile = lambda j: jnp.maximum(j - PROJ_NORM_STEPS, 0)
    return pl.pallas_call(
        _proj_kernel,
        out_shape=jax.ShapeDtypeStruct((n_rows, PROJ_COLS), BF16),
        grid=(n_rows // PROJ_TM, PROJ_NORM_STEPS + PROJ_COLS // PROJ_TN),
        in_specs=[
            pl.BlockSpec((PROJ_XB, D_MODEL),
                         lambda i, j: (i * PROJ_NORM_STEPS + jnp.minimum(j, PROJ_NORM_STEPS - 1), 0)),
            pl.BlockSpec((None, 1, D_MODEL), lambda i, j: (layer, 0, 0)),
            pl.BlockSpec((None, D_MODEL, PROJ_TN), lambda i, j: (layer, 0, col_tile(j))),
        ],
        out_specs=pl.BlockSpec((PROJ_TM, PROJ_TN), lambda i, j: (i, col_tile(j))),
        scratch_shapes=[pltpu.VMEM((PROJ_TM, D_MODEL), BF16)],
        compiler_params=pltpu.CompilerParams(
            dimension_semantics=("arbitrary", "arbitrary"), vmem_limit_bytes=PROJ_VMEM),
        name="proj",
    )(x2, norm_w, w_in)


def _rotary(x, swap, cos, sin):
    partner = jnp.dot(x, swap, preferred_element_type=F32)
    return x.astype(F32) * cos + partner * sin


def _tile_rows(tile, size):
    if isinstance(tile, int):
        return pl.ds(tile * size, size)
    return pl.ds(pl.multiple_of(tile * size, size), size)


def _attn_kernel(lamv_ref, sw_ref, tab_ref, q_ref, k_ref, v_ref, g_ref, o_ref,
                 kr_scr, vt_scr, qm_scr, s_scr, acc_scr, bm_scr, mask_scr, swap_scr, *, lam_init, seq):
    tk, tq = ATT_TK, ATT_TQ
    n_q = seq // tq
    nt = (((1,), (1,)), ((), ()))
    lo, hi = slice(0, tk), slice(tk, tq)

    def prep_q(tile, dst):
        rows = _tile_rows(tile, tq)
        qf = (_rotary(q_ref[0, rows, :], swap_scr[...], tab_ref[0, rows, :], tab_ref[1, rows, :])
              * (QK_DIM ** -0.5 * LOG2E))
        lane = lax.broadcasted_iota(jnp.int32, (tq, HEAD_W), 1)
        is_c1 = lane < QK_DIM
        qm_scr[dst, 0] = jnp.where(is_c1, qf, 0.0).astype(BF16)
        qm_scr[dst, 1] = jnp.where(is_c1, 0.0, qf).astype(BF16)

    def produce(kt, buf, q_slot, c, cols, masked=False):
        s = lax.dot_general(kr_scr[_tile_rows(kt, tk), :], qm_scr[q_slot, c, cols, :], nt,
                            preferred_element_type=F32)
        if masked:
            s = s + mask_scr[...]
        s_scr[buf, c, :, cols] = s
        return jnp.max(s, axis=0, keepdims=True)

    def consume(kt, buf, c, cols, m_old, bmax, masked=False):
        s = s_scr[buf, c, :, cols]
        if masked:
            s = s + mask_scr[...]
            bmax = jnp.max(s, axis=0, keepdims=True)
        m_new = jnp.maximum(m_old, bmax)
        p = jnp.exp2(s - m_new).astype(BF16)
        acc_scr[c, :, cols] = (jnp.exp2(m_old - m_new) * acc_scr[c, :, cols]
                               + jnp.dot(vt_scr[kt], p, preferred_element_type=F32))
        return m_new

    @pl.when((pl.program_id(0) == 0) & (pl.program_id(1) == 0))
    def _():
        row = lax.broadcasted_iota(jnp.int32, (tk, tk), 0)
        col = lax.broadcasted_iota(jnp.int32, (tk, tk), 1)
        mask_scr[...] = jnp.where(row <= col, 0.0, NEG_BIG)
        src = lax.broadcasted_iota(jnp.int32, (HEAD_W, HEAD_W), 0)
        dst = lax.broadcasted_iota(jnp.int32, (HEAD_W, HEAD_W), 1)
        swap_scr[...] = jnp.where(src == (dst ^ (QK_DIM // 2)), 1.0, 0.0).astype(BF16)

    def prep_kv(t):
        rows = slice(t * tk, (t + 1) * tk)
        vt_scr[t, :HEAD_W, :] = v_ref[0, rows, :].T
        vt_scr[t, HEAD_W:, :] = jnp.ones((VT_ROWS - HEAD_W, tk), BF16)
        kr_scr[rows, :] = _rotary(k_ref[0, rows, :], swap_scr[...], tab_ref[0, rows, :],
                                  tab_ref[1, rows, :]).astype(BF16)

    prep_kv(0)
    prep_q(0, 0)
    blocks = [(c, cols) for c in range(2) for cols in (lo, hi)]
    for t in range(1, seq // tk):
        if blocks:
            c, cols = blocks.pop(0)
            bm_scr[c, :, cols] = produce(0, 0, 0, c, cols)
        prep_kv(t)
    for c, cols in blocks:
        bm_scr[c, :, cols] = produce(0, 0, 0, c, cols)

    def q_tile(qi, _):
        slot = qi & 1
        acc_scr[...] = jnp.zeros_like(acc_scr)

        def produce_consume(kt_p, buf_p, kt_c, buf_c, stats, bmax_c):
            new_stats, new_bmax = [], []
            for i, (c, cols) in enumerate((c, cols) for c in range(2) for cols in (lo, hi)):
                new_bmax.append(produce(kt_p, buf_p, slot, c, cols))
                new_stats.append(consume(kt_c, buf_c, c, cols, stats[i], bmax_c[i]))
            return tuple(new_stats), tuple(new_bmax)

        def pair(j, carry):
            stats, bm_a = carry[:4], carry[4:]
            stats, bm_b = produce_consume(2 * j + 1, 1, 2 * j, 0, stats, bm_a)
            stats, bm_a = produce_consume(2 * j + 2, 0, 2 * j + 1, 1, stats, bm_b)
            return stats + bm_a

        n_full = 2 * qi
        stats0 = (jnp.full((1, tk), NEG_BIG, F32),) * 4
        bm0 = tuple(bm_scr[c, :, cols] for c in range(2) for cols in (lo, hi))
        carry = lax.fori_loop(0, qi // 2, lambda j, cr: pair(2 * j + 1, pair(2 * j, cr)), stats0 + bm0)
        carry = lax.cond(qi % 2 == 1, lambda cr: pair(qi - 1, cr), lambda cr: cr, carry)

        def tail(with_next):
            (m1_lo, m1_hi, m2_lo, m2_hi), (_, b1_hi, _, b2_hi) = carry[:4], carry[4:]

            def produce_next(c, cols):
                bm_scr[c, :, cols] = produce(0, 0, 1 - slot, c, cols)

            def epilogue(cols):
                rows = pl.ds(pl.multiple_of(qi * tq + cols.start, tk), tk)
                lamv = lamv_ref[...]
                lam = (jnp.exp(jnp.sum(lamv[0:1] * lamv[1:2], axis=-1, keepdims=True))
                       - jnp.exp(jnp.sum(lamv[2:3] * lamv[3:4], axis=-1, keepdims=True)) + lam_init)
                l1 = acc_scr[0, HEAD_W:HEAD_W + 1, cols]
                l2 = acc_scr[1, HEAD_W:HEAD_W + 1, cols]
                d = acc_scr[0, :HEAD_W, cols] * (1.0 / l1) - lam * (acc_scr[1, :HEAD_W, cols] * (1.0 / l2))
                ms = jnp.mean(d * d, axis=0, keepdims=True)
                y = (d * lax.rsqrt(ms + NORM_EPS)).T
                g = g_ref[0, rows, :].astype(F32)
                o_ref[0, rows, :] = (y * (sw_ref[...] * (1.0 - lam_init)) * _silu(g)).astype(BF16)

            b1_sub = produce(n_full + 1, 1, slot, 0, hi, masked=True)
            if with_next:
                prep_q(qi + 1, 1 - slot)
            consume(n_full, 0, 0, lo, m1_lo, None, masked=True)
            b2_sub = produce(n_full + 1, 1, slot, 1, hi, masked=True)
            m1_hi = consume(n_full, 0, 0, hi, m1_hi, b1_hi)
            if with_next:
                produce_next(0, lo)
            consume(n_full, 0, 1, lo, m2_lo, None, masked=True)
            if with_next:
                produce_next(0, hi)
            m2_hi = consume(n_full, 0, 1, hi, m2_hi, b2_hi)
            if with_next:
                produce_next(1, lo)
            epilogue(lo)
            consume(n_full + 1, 1, 0, hi, m1_hi, b1_sub)
            consume(n_full + 1, 1, 1, hi, m2_hi, b2_sub)
            if with_next:
                produce_next(1, hi)
            epilogue(hi)

        pl.when(qi < n_q - 1)(lambda: tail(True))
        pl.when(qi == n_q - 1)(lambda: tail(False))
        return 0

    lax.fori_loop(0, n_q, q_tile, 0)


def _attention(proj3, lamv, subln_w, tables, layer, lam_init):
    b, seq, _ = proj3.shape
    tk, tq = ATT_TK, ATT_TQ
    head_blk = lambda off: pl.BlockSpec((1, seq, HEAD_W), lambda bi, h: (bi, 0, off + h))
    return pl.pallas_call(
        functools.partial(_attn_kernel, lam_init=lam_init, seq=seq),
        out_shape=jax.ShapeDtypeStruct((b, seq, A_WIDTH), BF16),
        grid=(b, A_HEADS),
        in_specs=[
            pl.BlockSpec((None, 4, QK_DIM), lambda bi, h: (layer, 0, 0)),
            pl.BlockSpec((None, 1, HEAD_W), lambda bi, h: (layer, 0, 0)),
            pl.BlockSpec((2, seq, LANES), lambda bi, h: (0, 0, 0)),
            head_blk(0), head_blk(A_HEADS), head_blk(2 * A_HEADS), head_blk(3 * A_HEADS),
        ],
        out_specs=pl.BlockSpec((1, seq, HEAD_W), lambda bi, h: (bi, 0, h)),
        scratch_shapes=[
            pltpu.VMEM((seq, HEAD_W), BF16),
            pltpu.VMEM((seq // tk, VT_ROWS, tk), BF16),
            pltpu.VMEM((2, 2, tq, HEAD_W), BF16),
            pltpu.VMEM((2, 2, tk, tq), F32),
            pltpu.VMEM((2, VT_ROWS, tq), F32),
            pltpu.VMEM((2, 1, tq), F32),
            pltpu.VMEM((tk, tk), F32),
            pltpu.VMEM((HEAD_W, HEAD_W), BF16),
        ],
        compiler_params=pltpu.CompilerParams(
            dimension_semantics=("arbitrary", "arbitrary"), vmem_limit_bytes=ATTN_VMEM),
        name="diff_attn",
    )(lamv, subln_w, tables, proj3, proj3, proj3, proj3)


def _mix_kernel(ya_ref, u_ref, vs_ref, gb_ref, xc_ref, bg_ref, cg_ref, gc_ref, xcp_ref, cgp_ref,
                lng_ref, lnb_ref, ws_ref, bs_ref, cw_ref, wout_ref, x_ref, fnw_ref, o_ref,
                cat_scr, z_scr, *, is_last, tiles_per_seq):
    tm = MIX_TM
    n_chunks = tm // B_CHUNK
    i = pl.program_id(0)

    seq_start = (i % tiles_per_seq) == 0
    zp = cgp_ref[...].astype(F32) * xcp_ref[...].astype(F32)
    z_scr[0:HALO, :] = jnp.where(seq_start, 0.0, zp)
    z_scr[HALO:HALO + tm, :] = cg_ref[...].astype(F32) * xc_ref[...].astype(F32)

    tri_row = lax.broadcasted_iota(jnp.int32, (B_CHUNK, B_CHUNK), 0)
    tri_col = lax.broadcasted_iota(jnp.int32, (B_CHUNK, B_CHUNK), 1)
    w_tril = [jnp.where(tri_row >= tri_col, ws_ref[g], 0.0).astype(BF16) for g in range(B_GROUPS)]
    cw = cw_ref[...]

    def mix_b(c):
        rows = slice(c * B_CHUNK, (c + 1) * B_CHUNK)
        v = vs_ref[rows, :].astype(F32)
        mu = jnp.mean(v, axis=-1, keepdims=True)
        vc = v - mu
        var = jnp.mean(vc * vc, axis=-1, keepdims=True)
        vn = (vc * lax.rsqrt(var + LN_EPS) * lng_ref[...] + lnb_ref[...]).astype(BF16)
        for g in range(B_GROUPS):
            cols = slice(g * LANES, (g + 1) * LANES)
            mixed = jnp.dot(w_tril[g], vn[:, cols], preferred_element_type=F32) + bs_ref[g]
            yb = u_ref[rows, cols].astype(F32) * mixed * _silu(gb_ref[rows, cols].astype(F32))
            cat_scr[rows, cols] = yb.astype(BF16)

    def mix_c(cols):
        conv = cw[CONV_K - 1:CONV_K, cols] * z_scr[HALO:HALO + tm, cols]
        for tap in range(CONV_K - 1):
            start = HALO - (CONV_K - 1 - tap)
            conv = conv + cw[tap:tap + 1, cols] * z_scr[start:start + tm, cols]
        yc = bg_ref[:, cols].astype(F32) * conv * _silu(gc_ref[:, cols].astype(F32))
        cat_scr[:, B_WIDTH + cols.start:B_WIDTH + cols.stop] = yc.astype(BF16)

    cw_out = D_MODEL // n_chunks
    half_c = C_WIDTH // 2
    k_mid = A_WIDTH + B_WIDTH + half_c
    for c in range(n_chunks):
        cols = slice(c * cw_out, (c + 1) * cw_out)
        o_ref[:, cols] = x_ref[:, cols] + jnp.dot(ya_ref[...], wout_ref[0:A_WIDTH, cols],
                                                  preferred_element_type=F32)
        mix_b(c)
    mix_c(slice(0, half_c))
    for c in range(n_chunks):
        cols = slice(c * cw_out, (c + 1) * cw_out)
        o_ref[:, cols] += jnp.dot(cat_scr[:, 0:B_WIDTH + half_c], wout_ref[A_WIDTH:k_mid, cols],
                                  preferred_element_type=F32)
    mix_c(slice(half_c, C_WIDTH))
    out = o_ref[...] + jnp.dot(cat_scr[:, B_WIDTH + half_c:], wout_ref[k_mid:, :], preferred_element_type=F32)
    if is_last:
        ms = jnp.mean(out * out, axis=-1, keepdims=True)
        out = out * lax.rsqrt(ms + NORM_EPS) * fnw_ref[...]
    o_ref[...] = out


def _mix_out(ya2, proj2, x2, ln_g, ln_b, w_s, bs_b, conv_w, wout_bf16, layer, fnw, seq, is_last):
    n_rows = x2.shape[0]
    tm = MIX_TM
    base = (4 * A_WIDTH) // B_WIDTH
    pblk = lambda k: pl.BlockSpec((tm, B_WIDTH), lambda i: (i, base + k))
    halo = lambda k: pl.BlockSpec((HALO, B_WIDTH), lambda i: (jnp.maximum(i * (tm // HALO) - 1, 0), base + k))
    per_layer = lambda shape: pl.BlockSpec((None,) + shape, lambda i: (layer,) + (0,) * len(shape))
    return pl.pallas_call(
        functools.partial(_mix_kernel, is_last=is_last, tiles_per_seq=seq // tm),
        out_shape=jax.ShapeDtypeStruct((n_rows, D_MODEL), F32),
        grid=(n_rows // tm,),
        in_specs=[
            pl.BlockSpec((tm, A_WIDTH), lambda i: (i, 0)),
            pblk(0), pblk(1), pblk(2),
            pblk(3), pblk(4), pblk(5), pblk(6),
            halo(3), halo(5),
            per_layer((1, B_WIDTH)), per_layer((1, B_WIDTH)),
            per_layer((B_GROUPS, B_CHUNK, B_CHUNK)),
            per_layer((B_GROUPS, B_CHUNK, LANES)),
            per_layer((CONV_K, C_WIDTH)),
            per_layer((D_MODEL, D_MODEL)),
            pl.BlockSpec((tm, D_MODEL), lambda i: (i, 0)),
            pl.BlockSpec((1, D_MODEL), lambda i: (0, 0)),
        ],
        out_specs=pl.BlockSpec((tm, D_MODEL), lambda i: (i, 0)),
        scratch_shapes=[
            pltpu.VMEM((tm, B_WIDTH + C_WIDTH), BF16),
            pltpu.VMEM((HALO + tm, C_WIDTH), F32),
        ],
        compiler_params=pltpu.CompilerParams(
            dimension_semantics=("arbitrary",), vmem_limit_bytes=MIX_VMEM),
        name="mix_out",
    )(ya2, proj2, proj2, proj2, proj2, proj2, proj2, proj2, proj2, proj2,
      ln_g, ln_b, w_s, bs_b, conv_w, wout_bf16, x2, fnw)


def _rotary_tables(seq):
    half = QK_DIM // 2
    pos = jnp.arange(seq, dtype=F32)
    inv_freq = ROPE_THETA ** (-jnp.arange(0, QK_DIM, 2, dtype=F32) / QK_DIM)
    ang = pos[:, None] * inv_freq[None, :]
    cos = jnp.tile(jnp.cos(ang), (1, LANES // half))
    sin = jnp.tile(jnp.sin(ang), (1, LANES // half))
    sign = jnp.where(jnp.arange(LANES) % QK_DIM < half, -1.0, 1.0).astype(F32)
    return jnp.stack([cos, sin * sign])


def kernel(x, norm_w, w_in, lam_q1, lam_k1, lam_q2, lam_k2, subln_w, sgu_ln_g, sgu_ln_b, w_s, b_s,
           conv_w, w_out, final_norm_w):
    b, seq, d = x.shape
    depth = norm_w.shape[0]
    tables = _rotary_tables(seq)
    x2 = x.reshape(b * seq, d)
    fnw = final_norm_w.reshape(1, d)
    norm_w3 = norm_w.reshape(depth, 1, d)
    lamv = jnp.stack([lam_q1, lam_k1, lam_q2, lam_k2], axis=1)
    subln_w3 = subln_w.reshape(depth, 1, HEAD_W)
    ln_g3 = sgu_ln_g.reshape(depth, 1, B_WIDTH)
    ln_b3 = sgu_ln_b.reshape(depth, 1, B_WIDTH)
    bs_b = jnp.broadcast_to(b_s[..., None], (depth, B_GROUPS, B_CHUNK, LANES))
    w_out_bf16 = w_out.astype(BF16)
    for l in range(depth):
        lam_init = 0.8 - 0.6 * math.exp(-0.3 * l)
        proj2 = _project(x2, norm_w3, w_in, l)
        ya = _attention(proj2.reshape(b, seq, PROJ_COLS), lamv, subln_w3, tables, l, lam_init)
        x2 = _mix_out(ya.reshape(b * seq, A_WIDTH), proj2, x2, ln_g3, ln_b3, w_s, bs_b, conv_w,
                      w_out_bf16, l, fnw, seq, l == depth - 1)
    return x2.reshape(b, seq, d)
```
